```python
import jax, jax.numpy as jnp
from jax import lax
import numpy as np

D_MODEL = 1024
BATCH = 4
SEQ = 8192
DEPTH = 1

MIX_WIDTH = D_MODEL
CONV_WIDTH = MIX_WIDTH // 2
CONV_GROUPS = 8
CONV_GROUP_DIM = CONV_WIDTH // CONV_GROUPS
CONV_K = 3
SB_WIDTH = MIX_WIDTH - CONV_WIDTH
SB_HEADS = 8
SB_HEAD_DIM = SB_WIDTH // SB_HEADS
Q_BLOCK = 128
PROJ_WIDTH = 3 * CONV_WIDTH + 3 * SB_WIDTH
N_GROUPS = 4
EXPERTS_PER_GROUP = 8
N_EXPERTS = N_GROUPS * EXPERTS_PER_GROUP
TOP_K = 2
D_EXPERT = D_MODEL // 2
DISPATCH_BLOCK = 256
EPS = 1e-6

kernel_name = "hymba_conv_stickbreak_hiermoe"


def rms_norm(x, g):
    xf = x.astype(jnp.float32)
    y = xf * lax.rsqrt(jnp.mean(xf * xf, axis=-1, keepdims=True) + EPS)
    return (y * g.astype(jnp.float32)).astype(x.dtype)


def short_conv_mixer(gate_b, gate_c, h, conv_w):
    u = gate_c * h
    T = u.shape[1]
    up = jnp.pad(u, ((0, 0), (CONV_K - 1, 0), (0, 0)))
    y = sum(conv_w[i] * up[:, i:i + T] for i in range(CONV_K))
    return gate_b * y


def stick_breaking_attention(q, k, v):
    S = q.shape[2]
    scale = 1.0 / float(np.sqrt(SB_HEAD_DIM))
    outs = []
    for blk in range(S // Q_BLOCK):
        t0 = blk * Q_BLOCK
        L = t0 + Q_BLOCK
        qb = q[:, :, t0:L].astype(jnp.float32)
        kb = k[:, :, :L].astype(jnp.float32)
        vb = v[:, :, :L].astype(jnp.float32)
        z = jnp.einsum('bhqd,bhkd->bhqk', qb, kb) * scale
        t_idx = t0 + jnp.arange(Q_BLOCK)[:, None]
        s_idx = jnp.arange(L)[None, :]
        mask = s_idx < t_idx
        log_keep = jnp.where(mask, jax.nn.log_sigmoid(-z), 0.0)
        shifted = jnp.concatenate([log_keep[..., 1:], jnp.zeros_like(log_keep[..., :1])], axis=-1)
        suffix = lax.cumsum(shifted, axis=3, reverse=True)
        A = jnp.where(mask, jnp.exp(jax.nn.log_sigmoid(z) + suffix), 0.0)
        outs.append(jnp.einsum('bhqk,bhkd->bhqd', A, vb).astype(v.dtype))
    return jnp.concatenate(outs, axis=2)


def hierarchical_moe(xn, w_group, b_group, w_expert, b_expert, w1, w3, w2):
    Bsz, T, D = xn.shape
    N = Bsz * T
    xf = xn.reshape(N, D)
    g_logits = (xf @ w_group).astype(jnp.float32) + b_group.astype(jnp.float32)
    g_prob = jax.nn.softmax(g_logits, axis=-1)
    g_idx = jnp.argmax(g_logits, axis=-1).astype(jnp.int32)
    p_g = jnp.take_along_axis(g_prob, g_idx[:, None], axis=-1)
    e_logits = ((xf @ w_expert).astype(jnp.float32) + b_expert.astype(jnp.float32)).reshape(N, N_GROUPS, EXPERTS_PER_GROUP)
    e_in_group = jnp.take_along_axis(e_logits, g_idx[:, None, None], axis=1)[:, 0]
    top_v, top_i = lax.top_k(e_in_group, TOP_K)
    gate = (p_g * jax.nn.softmax(top_v, axis=-1)).reshape(-1)
    eid = (g_idx[:, None] * EXPERTS_PER_GROUP + top_i).reshape(-1).astype(jnp.int32)
    tok = jnp.repeat(jnp.arange(N, dtype=jnp.int32), TOP_K)
    NK = N * TOP_K
    e_s, tok_s, gate_s = lax.sort((eid, tok, gate), num_keys=1)
    counts = jnp.bincount(eid, length=N_EXPERTS).astype(jnp.int32)
    start = jnp.cumsum(counts) - counts
    pcounts = (counts + DISPATCH_BLOCK - 1) // DISPATCH_BLOCK * DISPATCH_BLOCK
    pend = jnp.cumsum(pcounts)
    pstart = pend - pcounts
    dest = pstart[e_s] + jnp.arange(NK, dtype=jnp.int32) - start[e_s]
    n_blocks = -(-NK // DISPATCH_BLOCK) + N_EXPERTS
    P = n_blocks * DISPATCH_BLOCK
    buf_tok = jnp.full((P,), N, dtype=jnp.int32).at[dest].set(tok_s)
    buf_gate = jnp.zeros((P,), jnp.float32).at[dest].set(gate_s)
    blk_start = jnp.arange(n_blocks, dtype=jnp.int32) * DISPATCH_BLOCK
    blk_e = jnp.clip(jnp.searchsorted(pend, blk_start, side='right'), 0, N_EXPERTS - 1).astype(jnp.int32)
    x_pad = jnp.concatenate([xf, jnp.zeros((1, D), xf.dtype)], axis=0)
    xin = x_pad[buf_tok].reshape(n_blocks, DISPATCH_BLOCK, D)

    def expert_block(args):
        xb, e = args
        h = jax.nn.silu(xb @ w1[e]) * (xb @ w3[e])
        return h @ w2[e]

    y = lax.map(expert_block, (xin, blk_e)).reshape(P, D).astype(jnp.float32)
    out = jnp.zeros((N + 1, D), jnp.float32).at[buf_tok].add(y * buf_gate[:, None])[:N]
    return out.astype(xn.dtype).reshape(Bsz, T, D)


def setup_inputs(seed: int = 0) -> dict:
    key = jax.random.key(seed)
    ks = jax.random.split(key, 20)
    f32 = jnp.float32
    nrm = lambda k, shape, s: jax.random.normal(k, shape, f32) * s
    gain = lambda k, shape: 1.0 + 0.02 * jax.random.normal(k, shape, f32)
    return {
        "x": jax.random.normal(ks[0], (BATCH, SEQ, D_MODEL), f32),
        "norm_mix_w": gain(ks[1], (DEPTH, D_MODEL)),
        "w_in": nrm(ks[2], (DEPTH, D_MODEL, PROJ_WIDTH), D_MODEL ** -0.5),
        "conv_w": nrm(ks[3], (DEPTH, CONV_K, CONV_WIDTH), CONV_K ** -0.5),
        "q_norm_w": gain(ks[4], (DEPTH, SB_HEAD_DIM)),
        "k_norm_w": gain(ks[5], (DEPTH, SB_HEAD_DIM)),
        "group_norm_conv_w": gain(ks[6], (DEPTH, CONV_GROUPS, CONV_GROUP_DIM)),
        "group_norm_attn_w": gain(ks[7], (DEPTH, SB_HEADS, SB_HEAD_DIM)),
        "w_out": nrm(ks[8], (DEPTH, MIX_WIDTH, D_MODEL), MIX_WIDTH ** -0.5),
        "norm_ffn_w": gain(ks[9], (DEPTH, D_MODEL)),
        "w_group": nrm(ks[10], (DEPTH, D_MODEL, N_GROUPS), D_MODEL ** -0.5),
        "b_group": nrm(ks[11], (DEPTH, N_GROUPS), 0.01),
        "w_expert": nrm(ks[12], (DEPTH, D_MODEL, N_EXPERTS), D_MODEL ** -0.5),
        "b_expert": nrm(ks[13], (DEPTH, N_EXPERTS), 0.01),
        "w1": nrm(ks[14], (DEPTH, N_EXPERTS, D_MODEL, D_EXPERT), D_MODEL ** -0.5),
        "w3": nrm(ks[15], (DEPTH, N_EXPERTS, D_MODEL, D_EXPERT), D_MODEL ** -0.5),
        "w2": nrm(ks[16], (DEPTH, N_EXPERTS, D_EXPERT, D_MODEL), D_EXPERT ** -0.5),
    }


def reference(x, norm_mix_w, w_in, conv_w, q_norm_w, k_norm_w, group_norm_conv_w, group_norm_attn_w, w_out, norm_ffn_w, w_group, b_group, w_expert, b_expert, w1, w3, w2):
    Bsz, S, _ = x.shape
    for l in range(DEPTH):
        xn = rms_norm(x, norm_mix_w[l])
        proj = xn @ w_in[l]
        c_b, c_c, c_h = jnp.split(proj[..., :3 * CONV_WIDTH], 3, axis=-1)
        q, k, v = jnp.split(proj[..., 3 * CONV_WIDTH:], 3, axis=-1)
        yc = short_conv_mixer(c_b, c_c, c_h, conv_w[l])
        yc = rms_norm(yc.reshape(Bsz, S, CONV_GROUPS, CONV_GROUP_DIM), group_norm_conv_w[l]).reshape(Bsz, S, CONV_WIDTH)
        to_heads = lambda t: t.reshape(Bsz, S, SB_HEADS, SB_HEAD_DIM)
        qh = rms_norm(to_heads(q), q_norm_w[l]).transpose(0, 2, 1, 3)
        kh = rms_norm(to_heads(k), k_norm_w[l]).transpose(0, 2, 1, 3)
        vh = to_heads(v).transpose(0, 2, 1, 3)
        ya = stick_breaking_attention(qh, kh, vh).transpose(0, 2, 1, 3)
        ya = rms_norm(ya, group_norm_attn_w[l]).reshape(Bsz, S, SB_WIDTH)
        x = x + jnp.concatenate([yc, ya], axis=-1) @ w_out[l]
        x = x + hierarchical_moe(rms_norm(x, norm_ffn_w[l]), w_group[l], b_group[l], w_expert[l], b_expert[l], w1[l], w3[l], w2[l])
    return x
```

```python
import functools

import numpy as np
import jax
import jax.numpy as jnp
from jax import lax
from jax.experimental import pallas as pl
from jax.experimental.pallas import tpu as pltpu

F32 = jnp.float32
BF16 = jnp.bfloat16

EPS = 1e-6
LANES = 128
HEAD_DIM = 64
N_HEADS = 8
N_CONV_GROUPS = 8
CONV_K = 3
N_GROUPS = 4
EXPERTS_PER_GROUP = 8
N_EXPERTS = N_GROUPS * EXPERTS_PER_GROUP
TOP_K = 2
DISPATCH_BLOCK = 256
HEAD_PAIRS = N_HEADS * HEAD_DIM // LANES

ROW_TILE = 8
KEY_BLOCK = 256
CUM_BLOCK = 128
VMEM_LIMIT = 56 * 1024 * 1024


def _pair_group_rms(x, lane_lo):
    x2 = x * x
    s_lo = jnp.sum(jnp.where(lane_lo, x2, 0.0), axis=-1, keepdims=True)
    s_hi = jnp.sum(jnp.where(lane_lo, 0.0, x2), axis=-1, keepdims=True)
    ms = jnp.where(lane_lo, s_lo, s_hi) * (1.0 / HEAD_DIM)
    return x * lax.rsqrt(ms + EPS)


def _proj_kernel(x_ref, g1_ref, win_ref, cw_ref, gq_ref, gk_ref, gc_ref,
                 yc_ref, q_ref, kt_ref, v_ref, ubuf, *, ts, width):
    i = pl.program_id(1)
    x = x_ref[0]
    ms = jnp.mean(x * x, axis=-1, keepdims=True)
    xn = (x * lax.rsqrt(ms + EPS) * g1_ref[...]).astype(BF16)

    def proj(c):
        return jnp.dot(xn, win_ref[:, c * width:(c + 1) * width], preferred_element_type=F32)

    lane_lo = lax.broadcasted_iota(jnp.int32, (ts, LANES), 1) < HEAD_DIM

    u = proj(1) * proj(2)

    @pl.when(i == 0)
    def _():
        ubuf[0:8, :] = jnp.zeros((8, width), F32)

    ubuf[8:8 + ts, :] = u
    u1 = ubuf[7:7 + ts, :]
    u2 = ubuf[6:6 + ts, :]
    ubuf[0:8, :] = u[ts - 8:ts, :]
    y = cw_ref[0:1, :] * u2 + cw_ref[1:2, :] * u1 + cw_ref[2:3, :] * u
    yc = proj(0) * y
    for c in range(width // LANES):
        sl = slice(c * LANES, (c + 1) * LANES)
        yc_ref[0, :, sl] = (_pair_group_rms(yc[:, sl], lane_lo) * gc_ref[:, sl]).astype(BF16)

    q = proj(3)
    for c in range(width // LANES):
        sl = slice(c * LANES, (c + 1) * LANES)
        q_ref[0, :, sl] = (_pair_group_rms(q[:, sl], lane_lo) * gq_ref[:, sl]).astype(BF16)

    k = proj(4)
    for c in range(width // LANES):
        sl = slice(c * LANES, (c + 1) * LANES)
        kn = _pair_group_rms(k[:, sl], lane_lo) * gk_ref[:, sl]
        knt = kn.T.astype(BF16)
        for j in range(ts // KEY_BLOCK):
            kt_ref[0, c, j] = knt[:, j * KEY_BLOCK:(j + 1) * KEY_BLOCK]

    v_ref[0] = proj(5).astype(BF16)


def _proj_call(x, g1, w_in, conv_w, gq, gk, gc, *, ts):
    B, S, D = x.shape
    width = w_in.shape[1] // 6
    nkb = S // KEY_BLOCK
    kern = functools.partial(_proj_kernel, ts=ts, width=width)
    const = lambda b, i: (0, 0)
    return pl.pallas_call(
        kern,
        grid=(B, S // ts),
        in_specs=[
            pl.BlockSpec((1, ts, D), lambda b, i: (b, i, 0)),
            pl.BlockSpec((1, D), const),
            pl.BlockSpec(w_in.shape, const),
            pl.BlockSpec(conv_w.shape, const),
            pl.BlockSpec((1, width), const),
            pl.BlockSpec((1, width), const),
            pl.BlockSpec((1, width), const),
        ],
        out_specs=[
            pl.BlockSpec((1, ts, width), lambda b, i: (b, i, 0)),
            pl.BlockSpec((1, ts, width), lambda b, i: (b, i, 0)),
            pl.BlockSpec((1, HEAD_PAIRS, ts // KEY_BLOCK, LANES, KEY_BLOCK), lambda b, i: (b, 0, i, 0, 0)),
            pl.BlockSpec((1, ts, width), lambda b, i: (b, i, 0)),
        ],
        out_shape=[
            jax.ShapeDtypeStruct((B, S, width), BF16),
            jax.ShapeDtypeStruct((B, S, width), BF16),
            jax.ShapeDtypeStruct((B, HEAD_PAIRS, nkb, LANES, KEY_BLOCK), BF16),
            jax.ShapeDtypeStruct((B, S, width), BF16),
        ],
        scratch_shapes=[pltpu.VMEM((ts + 8, width), F32)],
        compiler_params=pltpu.CompilerParams(
            dimension_semantics=("arbitrary", "arbitrary"), vmem_limit_bytes=VMEM_LIMIT),
        name="proj_conv_qknorm",
    )(x, g1, w_in, conv_w, gq, gk, gc)


def _attn_kernel(q_ref, kt_ref, v_ref, tm_ref, ga_ref, o_ref, acc_ref, carry_ref, *, tq):
    qi = pl.program_id(2)
    lane = lax.broadcasted_iota(jnp.int32, (tq, LANES), 1)
    lane_lo = lane < HEAD_DIM
    q = q_ref[0]
    zero = jnp.zeros_like(q)
    row = lax.broadcasted_iota(jnp.int32, (tq, CUM_BLOCK), 0)
    col = lax.broadcasted_iota(jnp.int32, (tq, CUM_BLOCK), 1)
    n_cum = KEY_BLOCK // CUM_BLOCK

    def key_block(qh, j, diagonal):
        z = jnp.dot(qh, kt_ref[0, 0, j], preferred_element_type=F32)
        a_parts = [None] * n_cum
        for c in reversed(range(n_cum)):
            zc = z[:, c * CUM_BLOCK:(c + 1) * CUM_BLOCK]
            sp = jnp.maximum(jnp.log(1.0 + jnp.exp(jnp.minimum(zc, 80.0))), zc)
            if diagonal:
                keep = (col + c * CUM_BLOCK) < row
                sp = jnp.where(keep, sp, 0.0)
            hi = sp.astype(BF16)
            lo = (sp - hi.astype(F32)).astype(BF16)
            r = jnp.dot(jnp.concatenate([hi, lo], axis=1), tm_ref[...], preferred_element_type=F32)
            carry = carry_ref[...]
            a = jnp.exp(zc - sp - (r[:, :CUM_BLOCK] + carry))
            if diagonal:
                a = jnp.where(keep, a, 0.0)
            a_parts[c] = a.astype(BF16)
            carry_ref[...] = carry + r[:, CUM_BLOCK:]
        a = jnp.concatenate(a_parts, axis=1)
        start = pl.multiple_of(j * KEY_BLOCK, KEY_BLOCK)
        acc_ref[...] += jnp.dot(a, v_ref[0, pl.ds(start, KEY_BLOCK), :], preferred_element_type=F32)

    outs = []
    for h in range(2):
        qh = jnp.where(lane_lo if h == 0 else jnp.logical_not(lane_lo), q, zero)
        acc_ref[...] = jnp.zeros_like(acc_ref)
        carry_ref[...] = jnp.zeros_like(carry_ref)
        key_block(qh, qi, True)

        def body(jj, _):
            key_block(qh, qi - 1 - jj, False)
            return 0

        lax.fori_loop(0, qi, body, 0)
        outs.append(acc_ref[...])
    o = jnp.where(lane_lo, outs[0], outs[1])
    o_ref[0] = (_pair_group_rms(o, lane_lo) * ga_ref[...]).astype(BF16)


def _attn_call(q, kt, v, tmat, ga, *, tq):
    B, S, width = q.shape
    assert tq == KEY_BLOCK
    kern = functools.partial(_attn_kernel, tq=tq)
    return pl.pallas_call(
        kern,
        grid=(B, HEAD_PAIRS, S // tq),
        in_specs=[
            pl.BlockSpec((1, tq, LANES), lambda b, p, i: (b, i, p)),
            pl.BlockSpec((1, 1) + kt.shape[2:], lambda b, p, i: (b, p, 0, 0, 0)),
            pl.BlockSpec((1, S, LANES), lambda b, p, i: (b, 0, p)),
            pl.BlockSpec(tmat.shape, lambda b, p, i: (0, 0)),
            pl.BlockSpec((1, LANES), lambda b, p, i: (0, p)),
        ],
        out_specs=pl.BlockSpec((1, tq, LANES), lambda b, p, i: (b, i, p)),
        out_shape=jax.ShapeDtypeStruct((B, S, width), BF16),
        scratch_shapes=[pltpu.VMEM((tq, LANES), F32), pltpu.VMEM((tq, LANES), F32)],
        compiler_params=pltpu.CompilerParams(
            dimension_semantics=("arbitrary", "arbitrary", "arbitrary"), vmem_limit_bytes=VMEM_LIMIT),
        name="stickbreak_attn",
    )(q, kt, v, tmat, ga)


def _mix_kernel(yc_ref, ya_ref, x_ref, wo_ref, g2_ref, wrh_ref, wrl_ref, br_ref, tri_ref,
                x1_ref, xn_ref, meta_ref, cnt_ref, *, tm, width):
    i = pl.program_id(0)
    mix = jnp.dot(yc_ref[...], wo_ref[0:width, :], preferred_element_type=F32)
    mix = mix + jnp.dot(ya_ref[...], wo_ref[width:2 * width, :], preferred_element_type=F32)
    x1 = x_ref[...] + mix
    x1_ref[...] = x1
    ms = jnp.mean(x1 * x1, axis=-1, keepdims=True)
    xn = x1 * lax.rsqrt(ms + EPS) * g2_ref[...]
    for c in range(ROW_TILE):
        xn_ref[:, c, :] = xn[:, c * LANES:(c + 1) * LANES]
    xh = xn.astype(BF16)
    xl = (xn - xh.astype(F32)).astype(BF16)
    logits = (jnp.dot(xh, wrh_ref[...], preferred_element_type=F32)
              + jnp.dot(xl, wrh_ref[...], preferred_element_type=F32)
              + jnp.dot(xh, wrl_ref[...], preferred_element_type=F32)) + br_ref[...]
    el = logits[:, :LANES]
    gl = logits[:, LANES:]
    lane = lax.broadcasted_iota(jnp.int32, (tm, LANES), 1)
    neg = jnp.float32(-jnp.inf)

    def first_argmax(vals):
        m = jnp.max(vals, axis=-1, keepdims=True)
        idx = jnp.min(jnp.where(vals == m, lane, LANES), axis=-1, keepdims=True)
        return m, idx

    glm = jnp.where(lane < N_GROUPS, gl, neg)
    gmax, gidx = first_argmax(glm)
    p_g = 1.0 / jnp.sum(jnp.exp(glm - gmax), axis=-1, keepdims=True)
    elm = jnp.where((lane // EXPERTS_PER_GROUP) == gidx, el, neg)
    v1, i1 = first_argmax(elm)
    v2, i2 = first_argmax(jnp.where(lane == i1, neg, elm))
    w = jnp.exp(v2 - v1)
    gate1 = p_g / (1.0 + w)
    gate2 = p_g * w / (1.0 + w)

    @pl.when(i == 0)
    def _():
        cnt_ref[...] = jnp.zeros_like(cnt_ref)

    o1 = lane == i1
    o2 = lane == i2
    osum = jnp.where(jnp.logical_or(o1, o2), 1.0, 0.0)
    base = jnp.dot(tri_ref[...], osum.astype(BF16), preferred_element_type=F32) + cnt_ref[...]
    rank1 = jnp.sum(jnp.where(o1, base, 0.0), axis=-1, keepdims=True)
    rank2 = jnp.sum(jnp.where(o2, base, 0.0), axis=-1, keepdims=True)
    cnt_ref[...] += jnp.sum(osum, axis=0, keepdims=True)

    cols = (i1.astype(F32), i2.astype(F32), rank1, rank2, gate1, gate2)
    meta = jnp.zeros((tm, LANES), F32)
    for n, cval in enumerate(cols):
        meta = jnp.where(lane == n, cval, meta)
    meta_ref[...] = meta


def _mix_call(yc, ya, x2d, w_out, g2, wr_hi, wr_lo, br, tri, *, tm):
    N, D = x2d.shape
    width = yc.shape[1]
    kern = functools.partial(_mix_kernel, tm=tm, width=width)
    row = lambda i: (i, 0)
    const = lambda i: (0, 0)
    return pl.pallas_call(
        kern,
        grid=(N // tm,),
        in_specs=[
            pl.BlockSpec((tm, width), row),
            pl.BlockSpec((tm, width), row),
            pl.BlockSpec((tm, D), row),
            pl.BlockSpec(w_out.shape, const),
            pl.BlockSpec((1, D), const),
            pl.BlockSpec(wr_hi.shape, const),
            pl.BlockSpec(wr_lo.shape, const),
            pl.BlockSpec(br.shape, const),
            pl.BlockSpec(tri.shape, const),
        ],
        out_specs=[
            pl.BlockSpec((tm, D), row),
            pl.BlockSpec((tm, ROW_TILE, LANES), lambda i: (i, 0, 0)),
            pl.BlockSpec((tm, LANES), row),
            pl.BlockSpec((1, LANES), const),
        ],
        out_shape=[
            jax.ShapeDtypeStruct((N, D), F32),
            jax.ShapeDtypeStruct((N, ROW_TILE, LANES), F32),
            jax.ShapeDtypeStruct((N, LANES), F32),
            jax.ShapeDtypeStruct((1, LANES), F32),
        ],
        compiler_params=pltpu.CompilerParams(
            dimension_semantics=("arbitrary",), vmem_limit_bytes=VMEM_LIMIT),
        name="outproj_router",
    )(yc, ya, x2d, w_out, g2, wr_hi, wr_lo, br, tri)


def _dispatch_kernel(dest_ref, xn_ref, zeros_ref, xin_ref, sem, *, td):
    del zeros_ref
    base = pl.program_id(0) * td

    def row_copy(r, k):
        return pltpu.make_async_copy(xn_ref.at[base + r], xin_ref.at[dest_ref[TOP_K * r + k]], sem)

    def start(r, _):
        for k in range(TOP_K):
            row_copy(r, k).start()
        return 0

    def wait(r, _):
        for k in range(TOP_K):
            row_copy(r, k).wait()
        return 0

    lax.fori_loop(0, td, start, 0)
    lax.fori_loop(0, td, wait, 0)


def _dispatch_call(dest_flat, xn, n_rows, *, td):
    N = xn.shape[0]
    zeros = jnp.zeros((n_rows,) + xn.shape[1:], xn.dtype)
    kern = functools.partial(_dispatch_kernel, td=td)
    return pl.pallas_call(
        kern,
        grid=(N // td,),
        in_specs=[
            pl.BlockSpec((TOP_K * td,), lambda i: (i,), memory_space=pltpu.SMEM),
            pl.BlockSpec(memory_space=pl.ANY),
            pl.BlockSpec(memory_space=pl.ANY),
        ],
        out_specs=pl.BlockSpec(memory_space=pl.ANY),
        out_shape=jax.ShapeDtypeStruct(zeros.shape, xn.dtype),
        scratch_shapes=[pltpu.SemaphoreType.DMA(())],
        input_output_aliases={2: 0},
        compiler_params=pltpu.CompilerParams(dimension_semantics=("arbitrary",)),
        name="moe_dispatch",
    )(dest_flat, xn, zeros)


def _ffn_kernel(blk_e_ref, n_used_ref, x_ref, w1_ref, w3_ref, w2_ref, y_ref, w1b, w3b, w2b):
    i = pl.program_id(0)
    prev = blk_e_ref[jnp.maximum(i - 1, 0)]
    changed = jnp.logical_or(i == 0, blk_e_ref[i] != prev)

    @pl.when(changed)
    def _():
        w1b[...] = w1_ref[...].astype(BF16)
        w3b[...] = w3_ref[...].astype(BF16)
        w2b[...] = w2_ref[...].astype(BF16)

    @pl.when(i < n_used_ref[0])
    def _():
        x = jnp.concatenate([x_ref[:, c, :] for c in range(ROW_TILE)], axis=1).astype(BF16)
        a = jnp.dot(x, w1b[...], preferred_element_type=F32)
        b = jnp.dot(x, w3b[...], preferred_element_type=F32)
        h = (a * jax.nn.sigmoid(a) * b).astype(BF16)
        y = jnp.dot(h, w2b[...], preferred_element_type=F32)
        for c in range(ROW_TILE):
            y_ref[:, c, :] = y[:, c * LANES:(c + 1) * LANES]

    @pl.when(i >= n_used_ref[0])
    def _():
        y_ref[...] = jnp.zeros_like(y_ref)


def _ffn_call(blk_e, n_used, xin, w1, w3, w2):
    P = xin.shape[0]
    E, D, DE = w1.shape
    n_blocks = P // DISPATCH_BLOCK
    tok_block = (DISPATCH_BLOCK, ROW_TILE, LANES)
    grid_spec = pltpu.PrefetchScalarGridSpec(
        num_scalar_prefetch=2,
        grid=(n_blocks,),
        in_specs=[
            pl.BlockSpec(tok_block, lambda i, be, nu: (i, 0, 0)),
            pl.BlockSpec((None, D, DE), lambda i, be, nu: (be[i], 0, 0)),
            pl.BlockSpec((None, D, DE), lambda i, be, nu: (be[i], 0, 0)),
            pl.BlockSpec((None, DE, D), lambda i, be, nu: (be[i], 0, 0)),
        ],
        out_specs=pl.BlockSpec(tok_block, lambda i, be, nu: (i, 0, 0)),
        scratch_shapes=[pltpu.VMEM((D, DE), BF16), pltpu.VMEM((D, DE), BF16), pltpu.VMEM((DE, D), BF16)],
    )
    return pl.pallas_call(
        _ffn_kernel,
        grid_spec=grid_spec,
        out_shape=jax.ShapeDtypeStruct(xin.shape, F32),
        compiler_params=pltpu.CompilerParams(
            dimension_semantics=("arbitrary",), vmem_limit_bytes=VMEM_LIMIT),
        name="moe_expert_ffn",
    )(blk_e, n_used, xin, w1, w3, w2)


def _combine_kernel(dest_ref, y_ref, x1_ref, meta_ref, o_ref, ybuf, sem, *, tc):
    def row_copy(r, k):
        return pltpu.make_async_copy(y_ref.at[dest_ref[TOP_K * r + k]], ybuf.at[k, r], sem)

    def start(r, _):
        for k in range(TOP_K):
            row_copy(r, k).start()
        return 0

    def wait(r, _):
        for k in range(TOP_K):
            row_copy(r, k).wait()
        return 0

    lax.fori_loop(0, tc, start, 0)
    lax.fori_loop(0, tc, wait, 0)
    meta = meta_ref[...]
    g1 = meta[:, 4:5]
    g2 = meta[:, 5:6]
    for c in range(ROW_TILE):
        sl = slice(c * LANES, (c + 1) * LANES)
        o_ref[:, sl] = x1_ref[:, sl] + (ybuf[0, :, c, :] * g1 + ybuf[1, :, c, :] * g2)


def _combine_call(dest_flat, y, x1, meta, *, tc):
    N, D = x1.shape
    kern = functools.partial(_combine_kernel, tc=tc)
    return pl.pallas_call(
        kern,
        grid=(N // tc,),
        in_specs=[
            pl.BlockSpec((TOP_K * tc,), lambda i: (i,), memory_space=pltpu.SMEM),
            pl.BlockSpec(memory_space=pl.ANY),
            pl.BlockSpec((tc, D), lambda i: (i, 0)),
            pl.BlockSpec((tc, LANES), lambda i: (i, 0)),
        ],
        out_specs=pl.BlockSpec((tc, D), lambda i: (i, 0)),
        out_shape=jax.ShapeDtypeStruct((N, D), F32),
        scratch_shapes=[pltpu.VMEM((TOP_K, tc, ROW_TILE, LANES), F32), pltpu.SemaphoreType.DMA(())],
        compiler_params=pltpu.CompilerParams(
            dimension_semantics=("arbitrary",), vmem_limit_bytes=VMEM_LIMIT),
        name="moe_combine",
    )(dest_flat, y, x1, meta)


def _suffix_matrix():
    r = np.arange(2 * CUM_BLOCK)[:, None] % CUM_BLOCK
    c = np.arange(2 * CUM_BLOCK)[None, :]
    m = np.where(c < CUM_BLOCK, r > c, True)
    return jnp.asarray(m, BF16)


def _layer(x, norm_mix_w, w_in, conv_w, q_norm_w, k_norm_w, gn_conv_w, gn_attn_w, w_out,
           norm_ffn_w, w_group, b_group, w_expert, b_expert, w1, w3, w2):
    B, S, D = x.shape
    N = B * S
    ts = min(512, S)
    tq = KEY_BLOCK
    tm = min(512, N)
    scale = 1.0 / float(np.sqrt(HEAD_DIM))

    gq = (jnp.tile(q_norm_w, N_HEADS) * scale)[None, :].astype(F32)
    gk = jnp.tile(k_norm_w, N_HEADS)[None, :].astype(F32)
    gc = gn_conv_w.reshape(1, -1).astype(F32)
    ga = gn_attn_w.reshape(1, -1).astype(F32)
    yc, q, kt, v = _proj_call(x, norm_mix_w[None, :], w_in.astype(BF16), conv_w, gq, gk, gc, ts=ts)
    ya = _attn_call(q, kt, v, _suffix_matrix(), ga, tq=tq)

    wr = jnp.zeros((D, 2 * LANES), F32)
    wr = wr.at[:, :N_EXPERTS].set(w_expert).at[:, LANES:LANES + N_GROUPS].set(w_group)
    br = jnp.zeros((1, 2 * LANES), F32)
    br = br.at[0, :N_EXPERTS].set(b_expert).at[0, LANES:LANES + N_GROUPS].set(b_group)
    wr_hi = wr.astype(BF16)
    wr_lo = (wr - wr_hi.astype(F32)).astype(BF16)
    tri = jnp.asarray(np.tril(np.ones((tm, tm), np.float32), -1), BF16)
    x1, xn2, meta, counts = _mix_call(
        yc.reshape(N, -1), ya.reshape(N, -1), x.reshape(N, D), w_out.astype(BF16),
        norm_ffn_w[None, :], wr_hi, wr_lo, br, tri, tm=tm)

    eid = meta[:, 0:TOP_K].astype(jnp.int32)
    rank = meta[:, TOP_K:2 * TOP_K].astype(jnp.int32)
    cnt = counts[0, :N_EXPERTS].astype(jnp.int32)
    pcnt = (cnt + DISPATCH_BLOCK - 1) // DISPATCH_BLOCK * DISPATCH_BLOCK
    pend = jnp.cumsum(pcnt)
    pstart = pend - pcnt
    dest = (pstart[eid] + rank).reshape(-1)
    n_blocks = -(-N * TOP_K // DISPATCH_BLOCK) + N_EXPERTS
    blk_start = jnp.arange(n_blocks, dtype=jnp.int32) * DISPATCH_BLOCK
    blk_e = jnp.clip(jnp.searchsorted(pend, blk_start, side='right'), 0, N_EXPERTS - 1).astype(jnp.int32)
    n_used = (pend[-1:] // DISPATCH_BLOCK).astype(jnp.int32)

    xin = _dispatch_call(dest, xn2, n_blocks * DISPATCH_BLOCK, td=min(512, N))
    y = _ffn_call(blk_e, n_used, xin, w1, w3, w2)
    out = _combine_call(dest, y, x1, meta, tc=min(256, N))
    return out.reshape(B, S, D)


def kernel(x, norm_mix_w, w_in, conv_w, q_norm_w, k_norm_w, group_norm_conv_w, group_norm_attn_w, w_out,
           norm_ffn_w, w_group, b_group, w_expert, b_expert, w1, w3, w2):
    depth = norm_mix_w.shape[0]
    for l in range(depth):
        x = _layer(x, norm_mix_w[l], w_in[l], conv_w[l], q_norm_w[l], k_norm_w[l], group_norm_conv_w[l],
                   group_norm_attn_w[l], w_out[l], norm_ffn_w[l], w_group[l], b_group[l], w_expert[l],
                   b_expert[l], w1[l], w3[l], w2[l])
    return x
```

```python
import functools

import numpy as np
import jax
import jax.numpy as jnp
from jax import lax
from jax.experimental import pallas as pl
from jax.experimental.pallas import tpu as pltpu

F32 = jnp.float32
BF16 = jnp.bfloat16

EPS = 1e-6
LANES = 128
HEAD_DIM = 64
N_HEADS = 8
N_CONV_GROUPS = 8
CONV_K = 3
N_GROUPS = 4
EXPERTS_PER_GROUP = 8
N_EXPERTS = N_GROUPS * EXPERTS_PER_GROUP
TOP_K = 2
DISPATCH_BLOCK = 256
HEAD_PAIRS = N_HEADS * HEAD_DIM // LANES

ROW_TILE = 8
KEY_BLOCK = 256
CUM_BLOCK = 128
VMEM_LIMIT = 56 * 1024 * 1024


def _pair_group_rms(x, lane_lo):
    x2 = x * x
    s_lo = jnp.sum(jnp.where(lane_lo, x2, 0.0), axis=-1, keepdims=True)
    s_hi = jnp.sum(jnp.where(lane_lo, 0.0, x2), axis=-1, keepdims=True)
    ms = jnp.where(lane_lo, s_lo, s_hi) * (1.0 / HEAD_DIM)
    return x * lax.rsqrt(ms + EPS)


def _proj_kernel(x_ref, g1_ref, win_ref, cw_ref, gq_ref, gk_ref, gc_ref,
                 yc_ref, q_ref, kt_ref, v_ref, ubuf, *, ts, width):
    i = pl.program_id(1)
    x = x_ref[0]
    ms = jnp.mean(x * x, axis=-1, keepdims=True)
    xn = (x * lax.rsqrt(ms + EPS) * g1_ref[...]).astype(BF16)

    def proj(c):
        return jnp.dot(xn, win_ref[:, c * width:(c + 1) * width], preferred_element_type=F32)

    lane_lo = lax.broadcasted_iota(jnp.int32, (ts, LANES), 1) < HEAD_DIM

    u = proj(1) * proj(2)

    @pl.when(i == 0)
    def _():
        ubuf[0:8, :] = jnp.zeros((8, width), F32)

    ubuf[8:8 + ts, :] = u
    u1 = ubuf[7:7 + ts, :]
    u2 = ubuf[6:6 + ts, :]
    ubuf[0:8, :] = u[ts - 8:ts, :]
    y = cw_ref[0:1, :] * u2 + cw_ref[1:2, :] * u1 + cw_ref[2:3, :] * u
    yc = proj(0) * y
    for c in range(width // LANES):
        sl = slice(c * LANES, (c + 1) * LANES)
        yc_ref[0, :, sl] = (_pair_group_rms(yc[:, sl], lane_lo) * gc_ref[:, sl]).astype(BF16)

    q = proj(3)
    for c in range(width // LANES):
        sl = slice(c * LANES, (c + 1) * LANES)
        q_ref[0, :, sl] = (_pair_group_rms(q[:, sl], lane_lo) * gq_ref[:, sl]).astype(BF16)

    k = proj(4)
    for c in range(width // LANES):
        sl = slice(c * LANES, (c + 1) * LANES)
        kn = _pair_group_rms(k[:, sl], lane_lo) * gk_ref[:, sl]
        knt = kn.T.astype(BF16)
        for j in range(ts // KEY_BLOCK):
            kt_ref[0, c, j] = knt[:, j * KEY_BLOCK:(j + 1) * KEY_BLOCK]

    v_ref[0] = proj(5).astype(BF16)


def _proj_call(x, g1, w_in, conv_w, gq, gk, gc, *, ts):
    B, S, D = x.shape
    width = w_in.shape[1] // 6
    nkb = S // KEY_BLOCK
    kern = functools.partial(_proj_kernel, ts=ts, width=width)
    const = lambda b, i: (0, 0)
    return pl.pallas_call(
        kern,
        grid=(B, S // ts),
        in_specs=[
            pl.BlockSpec((1, ts, D), lambda b, i: (b, i, 0)),
            pl.BlockSpec((1, D), const),
            pl.BlockSpec(w_in.shape, const),
            pl.BlockSpec(conv_w.shape, const),
            pl.BlockSpec((1, width), const),
            pl.BlockSpec((1, width), const),
            pl.BlockSpec((1, width), const),
        ],
        out_specs=[
            pl.BlockSpec((1, ts, width), lambda b, i: (b, i, 0)),
            pl.BlockSpec((1, ts, width), lambda b, i: (b, i, 0)),
            pl.BlockSpec((1, HEAD_PAIRS, ts // KEY_BLOCK, LANES, KEY_BLOCK), lambda b, i: (b, 0, i, 0, 0)),
            pl.BlockSpec((1, ts, width), lambda b, i: (b, i, 0)),
        ],
        out_shape=[
            jax.ShapeDtypeStruct((B, S, width), BF16),
            jax.ShapeDtypeStruct((B, S, width), BF16),
            jax.ShapeDtypeStruct((B, HEAD_PAIRS, nkb, LANES, KEY_BLOCK), BF16),
            jax.ShapeDtypeStruct((B, S, width), BF16),
        ],
        scratch_shapes=[pltpu.VMEM((ts + 8, width), F32)],
        compiler_params=pltpu.CompilerParams(
            dimension_semantics=("arbitrary", "arbitrary"), vmem_limit_bytes=VMEM_LIMIT),
        name="proj_conv_qknorm",
    )(x, g1, w_in, conv_w, gq, gk, gc)


def _attn_kernel(q_ref, kt_ref, v_ref, tm_ref, ga_ref, o_ref, acc_ref, carry_ref, *, tq):
    qi = pl.program_id(2)
    lane_lo = lax.broadcasted_iota(jnp.int32, (tq, LANES), 1) < HEAD_DIM
    q = q_ref[0]
    zero = jnp.zeros_like(q)
    qh = (jnp.where(lane_lo, q, zero), jnp.where(lane_lo, zero, q))
    n_cum = KEY_BLOCK // CUM_BLOCK
    n_sub = tq // KEY_BLOCK
    acc_ref[...] = jnp.zeros_like(acc_ref)
    carry_ref[...] = jnp.zeros_like(carry_ref)

    def key_block(j, row_lo, diag_off):
        rows = tq - row_lo
        kt = kt_ref[0, 0, j]
        start = pl.multiple_of(j * KEY_BLOCK, KEY_BLOCK)
        vb = v_ref[0, pl.ds(start, KEY_BLOCK), :]
        if diag_off is not None:
            row = lax.broadcasted_iota(jnp.int32, (rows, CUM_BLOCK), 0) + row_lo
            col = lax.broadcasted_iota(jnp.int32, (rows, CUM_BLOCK), 1) + diag_off
        for h in range(2):
            z = jnp.dot(qh[h][row_lo:, :], kt, preferred_element_type=F32)
            a_parts = [None] * n_cum
            for c in reversed(range(n_cum)):
                zc = z[:, c * CUM_BLOCK:(c + 1) * CUM_BLOCK]
                sp = jnp.maximum(jnp.log(1.0 + jnp.exp(jnp.minimum(zc, 80.0))), zc)
                if diag_off is not None:
                    keep = (col + c * CUM_BLOCK) < row
                    sp = jnp.where(keep, sp, 0.0)
                hi = sp.astype(BF16)
                lo = (sp - hi.astype(F32)).astype(BF16)
                r = jnp.dot(jnp.concatenate([hi, lo], axis=1), tm_ref[...], preferred_element_type=F32)
                carry = carry_ref[h, row_lo:, :]
                a = jnp.exp(zc - sp - (r[:, :CUM_BLOCK] + carry))
                if diag_off is not None:
                    a = jnp.where(keep, a, 0.0)
                a_parts[c] = a.astype(BF16)
                carry_ref[h, row_lo:, :] = carry + r[:, CUM_BLOCK:]
            a = jnp.concatenate(a_parts, axis=1)
            acc_ref[h, row_lo:, :] += jnp.dot(a, vb, preferred_element_type=F32)

    for s in reversed(range(n_sub)):
        key_block(qi * n_sub + s, s * KEY_BLOCK, s * KEY_BLOCK)

    def body(jj, _):
        key_block(qi * n_sub - 1 - jj, 0, None)
        return 0

    lax.fori_loop(0, qi * n_sub, body, 0)
    o = jnp.where(lane_lo, acc_ref[0], acc_ref[1])
    o_ref[0] = (_pair_group_rms(o, lane_lo) * ga_ref[...]).astype(BF16)


def _attn_call(q, kt, v, tmat, ga, *, tq):
    B, S, width = q.shape
    assert tq % KEY_BLOCK == 0 and S % tq == 0
    kern = functools.partial(_attn_kernel, tq=tq)
    return pl.pallas_call(
        kern,
        grid=(B, HEAD_PAIRS, S // tq),
        in_specs=[
            pl.BlockSpec((1, tq, LANES), lambda b, p, i: (b, i, p)),
            pl.BlockSpec((1, 1) + kt.shape[2:], lambda b, p, i: (b, p, 0, 0, 0)),
            pl.BlockSpec((1, S, LANES), lambda b, p, i: (b, 0, p)),
            pl.BlockSpec(tmat.shape, lambda b, p, i: (0, 0)),
            pl.BlockSpec((1, LANES), lambda b, p, i: (0, p)),
        ],
        out_specs=pl.BlockSpec((1, tq, LANES), lambda b, p, i: (b, i, p)),
        out_shape=jax.ShapeDtypeStruct((B, S, width), BF16),
        scratch_shapes=[pltpu.VMEM((2, tq, LANES), F32), pltpu.VMEM((2, tq, LANES), F32)],
        compiler_params=pltpu.CompilerParams(
            dimension_semantics=("arbitrary", "arbitrary", "arbitrary"), vmem_limit_bytes=VMEM_LIMIT),
        name="stickbreak_attn",
    )(q, kt, v, tmat, ga)


def _mix_kernel(yc_ref, ya_ref, x_ref, wo_ref, g2_ref, wrh_ref, wrl_ref, br_ref, tri_ref,
                x1_ref, xn_ref, meta_ref, cnt_ref, *, tm, width):
    i = pl.program_id(0)
    mix = jnp.dot(yc_ref[...], wo_ref[0:width, :], preferred_element_type=F32)
    mix = mix + jnp.dot(ya_ref[...], wo_ref[width:2 * width, :], preferred_element_type=F32)
    x1 = x_ref[...] + mix
    x1_ref[...] = x1
    ms = jnp.mean(x1 * x1, axis=-1, keepdims=True)
    xn = x1 * lax.rsqrt(ms + EPS) * g2_ref[...]
    for c in range(ROW_TILE):
        xn_ref[:, c, :] = xn[:, c * LANES:(c + 1) * LANES]
    xh = xn.astype(BF16)
    xl = (xn - xh.astype(F32)).astype(BF16)
    logits = (jnp.dot(xh, wrh_ref[...], preferred_element_type=F32)
              + jnp.dot(xl, wrh_ref[...], preferred_element_type=F32)
              + jnp.dot(xh, wrl_ref[...], preferred_element_type=F32)) + br_ref[...]
    el = logits[:, :LANES]
    gl = logits[:, LANES:]
    lane = lax.broadcasted_iota(jnp.int32, (tm, LANES), 1)
    neg = jnp.float32(-jnp.inf)

    def first_argmax(vals):
        m = jnp.max(vals, axis=-1, keepdims=True)
        idx = jnp.min(jnp.where(vals == m, lane, LANES), axis=-1, keepdims=True)
        return m, idx

    glm = jnp.where(lane < N_GROUPS, gl, neg)
    gmax, gidx = first_argmax(glm)
    p_g = 1.0 / jnp.sum(jnp.exp(glm - gmax), axis=-1, keepdims=True)
    elm = jnp.where((lane // EXPERTS_PER_GROUP) == gidx, el, neg)
    v1, i1 = first_argmax(elm)
    v2, i2 = first_argmax(jnp.where(lane == i1, neg, elm))
    w = jnp.exp(v2 - v1)
    gate1 = p_g / (1.0 + w)
    gate2 = p_g * w / (1.0 + w)

    @pl.when(i == 0)
    def _():
        cnt_ref[...] = jnp.zeros_like(cnt_ref)

    o1 = lane == i1
    o2 = lane == i2
    osum = jnp.where(jnp.logical_or(o1, o2), 1.0, 0.0)
    base = jnp.dot(tri_ref[...], osum.astype(BF16), preferred_element_type=F32) + cnt_ref[...]
    rank1 = jnp.sum(jnp.where(o1, base, 0.0), axis=-1, keepdims=True)
    rank2 = jnp.sum(jnp.where(o2, base, 0.0), axis=-1, keepdims=True)
    cnt_ref[...] += jnp.sum(osum, axis=0, keepdims=True)

    cols = (i1.astype(F32), i2.astype(F32), rank1, rank2, gate1, gate2)
    meta = jnp.zeros((tm, LANES), F32)
    for n, cval in enumerate(cols):
        meta = jnp.where(lane == n, cval, meta)
    meta_ref[...] = meta


def _mix_call(yc, ya, x2d, w_out, g2, wr_hi, wr_lo, br, tri, *, tm):
    N, D = x2d.shape
    width = yc.shape[1]
    kern = functools.partial(_mix_kernel, tm=tm, width=width)
    row = lambda i: (i, 0)
    const = lambda i: (0, 0)
    return pl.pallas_call(
        kern,
        grid=(N // tm,),
        in_specs=[
            pl.BlockSpec((tm, width), row),
            pl.BlockSpec((tm, width), row),
            pl.BlockSpec((tm, D), row),
            pl.BlockSpec(w_out.shape, const),
            pl.BlockSpec((1, D), const),
            pl.BlockSpec(wr_hi.shape, const),
            pl.BlockSpec(wr_lo.shape, const),
            pl.BlockSpec(br.shape, const),
            pl.BlockSpec(tri.shape, const),
        ],
        out_specs=[
            pl.BlockSpec((tm, D), row),
            pl.BlockSpec((tm, ROW_TILE, LANES), lambda i: (i, 0, 0)),
            pl.BlockSpec((tm, LANES), row),
            pl.BlockSpec((1, LANES), const),
        ],
        out_shape=[
            jax.ShapeDtypeStruct((N, D), F32),
            jax.ShapeDtypeStruct((N, ROW_TILE, LANES), F32),
            jax.ShapeDtypeStruct((N, LANES), F32),
            jax.ShapeDtypeStruct((1, LANES), F32),
        ],
        compiler_params=pltpu.CompilerParams(
            dimension_semantics=("arbitrary",), vmem_limit_bytes=VMEM_LIMIT),
        name="outproj_router",
    )(yc, ya, x2d, w_out, g2, wr_hi, wr_lo, br, tri)


def _dispatch_kernel(pstart_ref, cnt_ref, pend_ref, eid_ref, rank_ref, xn_ref, xin_ref, zrow, sem, zsem, *, td):
    @pl.when(pl.program_id(0) == 0)
    def _():
        zrow[...] = jnp.zeros_like(zrow)

        def pad_rows(fn):
            def per_expert(e, _):
                lax.fori_loop(pstart_ref[e] + cnt_ref[e], pend_ref[e], fn, 0)
                return 0
            lax.fori_loop(0, N_EXPERTS, per_expert, 0)
            lax.fori_loop(pend_ref[N_EXPERTS - 1], xin_ref.shape[0], fn, 0)

        def start(p, _):
            pltpu.make_async_copy(zrow, xin_ref.at[p], zsem).start()
            return 0

        def wait(p, _):
            pltpu.make_async_copy(zrow, xin_ref.at[p], zsem).wait()
            return 0

        pad_rows(start)
        pad_rows(wait)

    def row_copy(r, k):
        n = TOP_K * r + k
        return pltpu.make_async_copy(xn_ref.at[r], xin_ref.at[pstart_ref[eid_ref[n]] + rank_ref[n]], sem)

    def start_rows(r, _):
        for k in range(TOP_K):
            row_copy(r, k).start()
        return 0

    def wait_rows(r, _):
        for k in range(TOP_K):
            row_copy(r, k).wait()
        return 0

    lax.fori_loop(0, td, start_rows, 0)
    lax.fori_loop(0, td, wait_rows, 0)


def _dispatch_call(pstart, cnt, pend, eid_flat, rank_flat, xn, n_rows, *, td):
    N = xn.shape[0]
    kern = functools.partial(_dispatch_kernel, td=td)
    smem_block = pl.BlockSpec((TOP_K * td,), lambda i, *_: (i,), memory_space=pltpu.SMEM)
    grid_spec = pltpu.PrefetchScalarGridSpec(
        num_scalar_prefetch=3,
        grid=(N // td,),
        in_specs=[smem_block, smem_block, pl.BlockSpec((td, ROW_TILE, LANES), lambda i, *_: (i, 0, 0))],
        out_specs=pl.BlockSpec(memory_space=pl.ANY),
        scratch_shapes=[pltpu.VMEM((ROW_TILE, LANES), xn.dtype), pltpu.SemaphoreType.DMA(()),
                        pltpu.SemaphoreType.DMA(())],
    )
    return pl.pallas_call(
        kern,
        grid_spec=grid_spec,
        out_shape=jax.ShapeDtypeStruct((n_rows,) + xn.shape[1:], xn.dtype),
        compiler_params=pltpu.CompilerParams(dimension_semantics=("arbitrary",)),
        name="moe_dispatch",
    )(pstart, cnt, pend, eid_flat, rank_flat, xn)


def _ffn_kernel(blk_e_ref, n_used_ref, x_ref, w1_ref, w3_ref, w2_ref, y_ref, w1b, w3b, w2b):
    i = pl.program_id(0)
    prev = blk_e_ref[jnp.maximum(i - 1, 0)]
    changed = jnp.logical_or(i == 0, blk_e_ref[i] != prev)

    @pl.when(changed)
    def _():
        w1b[...] = w1_ref[...].astype(BF16)
        w3b[...] = w3_ref[...].astype(BF16)
        w2b[...] = w2_ref[...].astype(BF16)

    @pl.when(i < n_used_ref[0])
    def _():
        x = jnp.concatenate([x_ref[:, c, :] for c in range(ROW_TILE)], axis=1).astype(BF16)
        a = jnp.dot(x, w1b[...], preferred_element_type=F32)
        b = jnp.dot(x, w3b[...], preferred_element_type=F32)
        h = (a * jax.nn.sigmoid(a) * b).astype(BF16)
        y = jnp.dot(h, w2b[...], preferred_element_type=F32)
        for c in range(ROW_TILE):
            y_ref[:, c, :] = y[:, c * LANES:(c + 1) * LANES]

    @pl.when(i >= n_used_ref[0])
    def _():
        y_ref[...] = jnp.zeros_like(y_ref)


def _ffn_call(blk_e, n_used, xin, w1, w3, w2):
    P = xin.shape[0]
    E, D, DE = w1.shape
    n_blocks = P // DISPATCH_BLOCK
    tok_block = (DISPATCH_BLOCK, ROW_TILE, LANES)
    grid_spec = pltpu.PrefetchScalarGridSpec(
        num_scalar_prefetch=2,
        grid=(n_blocks,),
        in_specs=[
            pl.BlockSpec(tok_block, lambda i, be, nu: (jnp.minimum(i, nu[0] - 1), 0, 0)),
            pl.BlockSpec((None, D, DE), lambda i, be, nu: (be[i], 0, 0)),
            pl.BlockSpec((None, D, DE), lambda i, be, nu: (be[i], 0, 0)),
            pl.BlockSpec((None, DE, D), lambda i, be, nu: (be[i], 0, 0)),
        ],
        out_specs=pl.BlockSpec(tok_block, lambda i, be, nu: (i, 0, 0)),
        scratch_shapes=[pltpu.VMEM((D, DE), BF16), pltpu.VMEM((D, DE), BF16), pltpu.VMEM((DE, D), BF16)],
    )
    return pl.pallas_call(
        _ffn_kernel,
        grid_spec=grid_spec,
        out_shape=jax.ShapeDtypeStruct(xin.shape, F32),
        compiler_params=pltpu.CompilerParams(
            dimension_semantics=("arbitrary",), vmem_limit_bytes=VMEM_LIMIT),
        name="moe_expert_ffn",
    )(blk_e, n_used, xin, w1, w3, w2)


def _combine_kernel(pstart_ref, eid_ref, rank_ref, eid_next_ref, rank_next_ref, y_ref, x1_ref, meta_ref,
                    o_ref, ybuf, sems, *, tc):
    i = pl.program_id(0)
    slot = i % 2

    def gather(e_ref, r_ref, s):
        def start(r, _):
            for k in range(TOP_K):
                n = TOP_K * r + k
                pltpu.make_async_copy(
                    y_ref.at[pstart_ref[e_ref[n]] + r_ref[n]], ybuf.at[s, k, r], sems.at[s]).start()
            return 0
        lax.fori_loop(0, tc, start, 0)

    @pl.when(i == 0)
    def _():
        gather(eid_ref, rank_ref, 0)

    @pl.when(i + 1 < pl.num_programs(0))
    def _():
        gather(eid_next_ref, rank_next_ref, 1 - slot)

    def wait(r, _):
        for k in range(TOP_K):
            pltpu.make_async_copy(y_ref.at[0], ybuf.at[slot, k, r], sems.at[slot]).wait()
        return 0

    lax.fori_loop(0, tc, wait, 0)
    meta = meta_ref[...]
    g1 = meta[:, 4:5]
    g2 = meta[:, 5:6]
    for c in range(ROW_TILE):
        sl = slice(c * LANES, (c + 1) * LANES)
        o_ref[:, sl] = x1_ref[:, sl] + (ybuf[slot, 0, :, c, :] * g1 + ybuf[slot, 1, :, c, :] * g2)


def _combine_call(pstart, eid_flat, rank_flat, y, x1, meta, *, tc):
    N, D = x1.shape
    n_tiles = N // tc
    kern = functools.partial(_combine_kernel, tc=tc)
    cur = pl.BlockSpec((TOP_K * tc,), lambda i, *_: (i,), memory_space=pltpu.SMEM)
    nxt = pl.BlockSpec((TOP_K * tc,), lambda i, *_: (jnp.minimum(i + 1, n_tiles - 1),), memory_space=pltpu.SMEM)
    grid_spec = pltpu.PrefetchScalarGridSpec(
        num_scalar_prefetch=1,
        grid=(n_tiles,),
        in_specs=[
            cur, cur, nxt, nxt,
            pl.BlockSpec(memory_space=pl.ANY),
            pl.BlockSpec((tc, D), lambda i, *_: (i, 0)),
            pl.BlockSpec((tc, LANES), lambda i, *_: (i, 0)),
        ],
        out_specs=pl.BlockSpec((tc, D), lambda i, *_: (i, 0)),
        scratch_shapes=[pltpu.VMEM((2, TOP_K, tc, ROW_TILE, LANES), F32), pltpu.SemaphoreType.DMA((2,))],
    )
    return pl.pallas_call(
        kern,
        grid_spec=grid_spec,
        out_shape=jax.ShapeDtypeStruct((N, D), F32),
        compiler_params=pltpu.CompilerParams(
            dimension_semantics=("arbitrary",), vmem_limit_bytes=VMEM_LIMIT),
        name="moe_combine",
    )(pstart, eid_flat, rank_flat, eid_flat, rank_flat, y, x1, meta)


def _suffix_matrix():
    r = np.arange(2 * CUM_BLOCK)[:, None] % CUM_BLOCK
    c = np.arange(2 * CUM_BLOCK)[None, :]
    m = np.where(c < CUM_BLOCK, r > c, True)
    return jnp.asarray(m, BF16)


def _layer(x, norm_mix_w, w_in, conv_w, q_norm_w, k_norm_w, gn_conv_w, gn_attn_w, w_out,
           norm_ffn_w, w_group, b_group, w_expert, b_expert, w1, w3, w2):
    B, S, D = x.shape
    N = B * S
    ts = min(512, S)
    tq = min(512, S)
    tm = min(512, N)
    scale = 1.0 / float(np.sqrt(HEAD_DIM))

    gq = (jnp.tile(q_norm_w, N_HEADS) * scale)[None, :].astype(F32)
    gk = jnp.tile(k_norm_w, N_HEADS)[None, :].astype(F32)
    gc = gn_conv_w.reshape(1, -1).astype(F32)
    ga = gn_attn_w.reshape(1, -1).astype(F32)
    yc, q, kt, v = _proj_call(x, norm_mix_w[None, :], w_in.astype(BF16), conv_w, gq, gk, gc, ts=ts)
    ya = _attn_call(q, kt, v, _suffix_matrix(), ga, tq=tq)

    pad_lanes = lambda a: jnp.pad(a.astype(F32), ((0, 0), (0, LANES - a.shape[1])))
    wr = jnp.concatenate([pad_lanes(w_expert), pad_lanes(w_group)], axis=1)
    br = jnp.concatenate([pad_lanes(b_expert[None, :]), pad_lanes(b_group[None, :])], axis=1)
    wr_hi = wr.astype(BF16)
    wr_lo = (wr - wr_hi.astype(F32)).astype(BF16)
    tri = jnp.asarray(np.tril(np.ones((tm, tm), np.float32), -1), BF16)
    x1, xn2, meta, counts = _mix_call(
        yc.reshape(N, -1), ya.reshape(N, -1), x.reshape(N, D), w_out.astype(BF16),
        norm_ffn_w[None, :], wr_hi, wr_lo, br, tri, tm=tm)

    eid = meta[:, 0:TOP_K].astype(jnp.int32).reshape(-1)
    rank = meta[:, TOP_K:2 * TOP_K].astype(jnp.int32).reshape(-1)
    cnt = counts[0, :N_EXPERTS].astype(jnp.int32)
    pcnt = (cnt + DISPATCH_BLOCK - 1) // DISPATCH_BLOCK * DISPATCH_BLOCK
    pend = jnp.cumsum(pcnt)
    pstart = pend - pcnt
    n_blocks = -(-N * TOP_K // DISPATCH_BLOCK) + N_EXPERTS
    blk_start = jnp.arange(n_blocks, dtype=jnp.int32) * DISPATCH_BLOCK
    blk_e = jnp.sum((blk_start[:, None] >= pend[None, :]).astype(jnp.int32), axis=1)
    blk_e = jnp.minimum(blk_e, N_EXPERTS - 1)
    n_used = pend[-1:] // DISPATCH_BLOCK

    xin = _dispatch_call(pstart, cnt, pend, eid, rank, xn2, n_blocks * DISPATCH_BLOCK, td=min(512, N))
    y = _ffn_call(blk_e, n_used, xin, w1, w3, w2)
    out = _combine_call(pstart, eid, rank, y, x1, meta, tc=min(256, N))
    return out.reshape(B, S, D)


def kernel(x, norm_mix_w, w_in, conv_w, q_norm_w, k_norm_w, group_norm_conv_w, group_norm_attn_w, w_out,
           norm_ffn_w, w_group, b_group, w_expert, b_expert, w1, w3, w2):
    depth = norm_mix_w.shape[0]
    for l in range(depth):
        x = _layer(x, norm_mix_w[l], w_in[l], conv_w[l], q_norm_w[l], k_norm_w[l], group_norm_conv_w[l],
                   group_norm_attn_w[l], w_out[l], norm_ffn_w[l], w_group[l], b_group[l], w_expert[l],
                   b_expert[l], w1[l], w3[l], w2[l])
    return x
```

```python
import functools

import numpy as np
import jax
import jax.numpy as jnp
from jax import lax
from jax.experimental import pallas as pl
from jax.experimental.pallas import tpu as pltpu

F32 = jnp.float32
BF16 = jnp.bfloat16

EPS = 1e-6
MASKED = -1e30
STICK_GONE = 110.0
LANES = 128
HEAD_DIM = 64
N_HEADS = 8
N_CONV_GROUPS = 8
CONV_K = 3
N_GROUPS = 4
EXPERTS_PER_GROUP = 8
N_EXPERTS = N_GROUPS * EXPERTS_PER_GROUP
TOP_K = 2
DISPATCH_BLOCK = 256
HEAD_PAIRS = N_HEADS * HEAD_DIM // LANES

ROW_TILE = 8
KEY_BLOCK = 256
DMA_UNROLL = 4
VMEM_LIMIT = 56 * 1024 * 1024


def _pair_group_rms(x, lane_lo):
    x2 = x * x
    s_lo = jnp.sum(jnp.where(lane_lo, x2, 0.0), axis=-1, keepdims=True)
    s_hi = jnp.sum(jnp.where(lane_lo, 0.0, x2), axis=-1, keepdims=True)
    ms = jnp.where(lane_lo, s_lo, s_hi) * (1.0 / HEAD_DIM)
    return x * lax.rsqrt(ms + EPS)


def _proj_kernel(x_ref, g1_ref, win_ref, cw_ref, gq_ref, gk_ref, gc_ref,
                 yc_ref, q_ref, kt_ref, v_ref, ubuf, *, ts, width):
    i = pl.program_id(1)
    x = x_ref[0]
    ms = jnp.mean(x * x, axis=-1, keepdims=True)
    xn = (x * lax.rsqrt(ms + EPS) * g1_ref[...]).astype(BF16)

    def proj(c):
        return jnp.dot(xn, win_ref[:, c * width:(c + 1) * width], preferred_element_type=F32)

    lane_lo = lax.broadcasted_iota(jnp.int32, (ts, LANES), 1) < HEAD_DIM

    u = proj(1) * proj(2)

    @pl.when(i == 0)
    def _():
        ubuf[0:8, :] = jnp.zeros((8, width), F32)

    ubuf[8:8 + ts, :] = u
    u1 = ubuf[7:7 + ts, :]
    u2 = ubuf[6:6 + ts, :]
    ubuf[0:8, :] = u[ts - 8:ts, :]
    y = cw_ref[0:1, :] * u2 + cw_ref[1:2, :] * u1 + cw_ref[2:3, :] * u
    yc = proj(0) * y
    for c in range(width // LANES):
        sl = slice(c * LANES, (c + 1) * LANES)
        yc_ref[0, :, sl] = (_pair_group_rms(yc[:, sl], lane_lo) * gc_ref[:, sl]).astype(BF16)

    q = proj(3)
    for c in range(width // LANES):
        sl = slice(c * LANES, (c + 1) * LANES)
        q_ref[0, :, sl] = (_pair_group_rms(q[:, sl], lane_lo) * gq_ref[:, sl]).astype(BF16)

    k = proj(4)
    for c in range(width // LANES):
        sl = slice(c * LANES, (c + 1) * LANES)
        kn = _pair_group_rms(k[:, sl], lane_lo) * gk_ref[:, sl]
        knt = kn.T.astype(BF16)
        for j in range(ts // KEY_BLOCK):
            kt_ref[0, c, j] = knt[:, j * KEY_BLOCK:(j + 1) * KEY_BLOCK]

    v_ref[0] = proj(5).astype(BF16)


def _proj_call(x, g1, w_in, conv_w, gq, gk, gc, *, ts):
    B, S, D = x.shape
    width = w_in.shape[1] // 6
    nkb = S // KEY_BLOCK
    kern = functools.partial(_proj_kernel, ts=ts, width=width)
    const = lambda b, i: (0, 0)
    return pl.pallas_call(
        kern,
        grid=(B, S // ts),
        in_specs=[
            pl.BlockSpec((1, ts, D), lambda b, i: (b, i, 0)),
            pl.BlockSpec((1, D), const),
            pl.BlockSpec(w_in.shape, const),
            pl.BlockSpec(conv_w.shape, const),
            pl.BlockSpec((1, width), const),
            pl.BlockSpec((1, width), const),
            pl.BlockSpec((1, width), const),
        ],
        out_specs=[
            pl.BlockSpec((1, ts, width), lambda b, i: (b, i, 0)),
            pl.BlockSpec((1, ts, width), lambda b, i: (b, i, 0)),
            pl.BlockSpec((1, HEAD_PAIRS, ts // KEY_BLOCK, LANES, KEY_BLOCK), lambda b, i: (b, 0, i, 0, 0)),
            pl.BlockSpec((1, ts, width), lambda b, i: (b, i, 0)),
        ],
        out_shape=[
            jax.ShapeDtypeStruct((B, S, width), BF16),
            jax.ShapeDtypeStruct((B, S, width), BF16),
            jax.ShapeDtypeStruct((B, HEAD_PAIRS, nkb, LANES, KEY_BLOCK), BF16),
            jax.ShapeDtypeStruct((B, S, width), BF16),
        ],
        scratch_shapes=[pltpu.VMEM((ts + 8, width), F32)],
        compiler_params=pltpu.CompilerParams(
            dimension_semantics=("arbitrary", "arbitrary"), vmem_limit_bytes=VMEM_LIMIT),
        name="proj_conv_qknorm",
    )(x, g1, w_in, conv_w, gq, gk, gc)


def _attn_kernel(q_ref, kt_ref, v_ref, tm_ref, ga_ref, o_ref, acc_ref, carry_ref, *, tq):
    qi = pl.program_id(2)
    lane_lo = lax.broadcasted_iota(jnp.int32, (tq, LANES), 1) < HEAD_DIM
    q = q_ref[0]
    zero = jnp.zeros_like(q)
    qh = (jnp.where(lane_lo, q, zero), jnp.where(lane_lo, zero, q))
    acc_ref[...] = jnp.zeros_like(acc_ref)
    carry_ref[...] = jnp.zeros_like(carry_ref)

    n_sub = tq // KEY_BLOCK

    def key_block(j, s, diagonal):
        rows = slice(s * KEY_BLOCK, (s + 1) * KEY_BLOCK)
        kt = kt_ref[0, 0, j]
        start = pl.multiple_of(j * KEY_BLOCK, KEY_BLOCK)
        vb = v_ref[0, pl.ds(start, KEY_BLOCK), :]
        for h in range(2):
            z = jnp.dot(qh[h][rows, :], kt, preferred_element_type=F32)
            sp = jnp.maximum(jnp.log(1.0 + jnp.exp(jnp.minimum(z, 80.0))), z)
            if diagonal:
                row = lax.broadcasted_iota(jnp.int32, (KEY_BLOCK, KEY_BLOCK), 0)
                col = lax.broadcasted_iota(jnp.int32, (KEY_BLOCK, KEY_BLOCK), 1)
                sp = jnp.where(col < row, sp, 0.0)
                z = jnp.where(col < row, z, MASKED)
            r = jnp.dot(sp.astype(BF16), tm_ref[...], preferred_element_type=F32)
            carry = carry_ref[h, rows, :]
            a = jnp.exp(z - (r + jnp.concatenate([carry] * (KEY_BLOCK // LANES), axis=1)))
            carry_ref[h, rows, :] = carry + jnp.broadcast_to(r[:, 0:1], carry.shape)
            acc_ref[h, rows, :] += jnp.dot(a.astype(BF16), vb, preferred_element_type=F32)

    first = qi * n_sub

    @pl.when(qi == 0)
    def _():
        key_block(0, 0, True)
        for s in range(1, n_sub):
            key_block(s, s, True)
            key_block(s - 1, s, False)

    @pl.when(qi > 0)
    def _():
        for s in range(n_sub):
            key_block(first + s, s, True)
            key_block(first + s - 1, s, False)

    def stick_left(t):
        return jnp.logical_and(first + n_sub - 3 - t >= 0, jnp.min(carry_ref[...]) < STICK_GONE)

    def step(t):
        for s in range(n_sub):
            j = first + s - 2 - t

            @pl.when(j >= 0)
            def _():
                key_block(j, s, False)
        return t + 1

    lax.while_loop(stick_left, step, 0)

    o = jnp.where(lane_lo, acc_ref[0], acc_ref[1])
    o_ref[0] = (_pair_group_rms(o, lane_lo) * ga_ref[...]).astype(BF16)


def _attn_call(q, kt, v, tmat, ga, *, tq):
    B, S, width = q.shape
    assert tq % KEY_BLOCK == 0 and S % tq == 0
    kern = functools.partial(_attn_kernel, tq=tq)
    return pl.pallas_call(
        kern,
        grid=(B, HEAD_PAIRS, S // tq),
        in_specs=[
            pl.BlockSpec((1, tq, LANES), lambda b, p, i: (b, i, p)),
            pl.BlockSpec((1, 1) + kt.shape[2:], lambda b, p, i: (b, p, 0, 0, 0)),
            pl.BlockSpec((1, S, LANES), lambda b, p, i: (b, 0, p)),
            pl.BlockSpec(tmat.shape, lambda b, p, i: (0, 0)),
            pl.BlockSpec((1, LANES), lambda b, p, i: (0, p)),
        ],
        out_specs=pl.BlockSpec((1, tq, LANES), lambda b, p, i: (b, i, p)),
        out_shape=jax.ShapeDtypeStruct((B, S, width), BF16),
        scratch_shapes=[pltpu.VMEM((2, tq, LANES), F32), pltpu.VMEM((2, tq, LANES), F32)],
        compiler_params=pltpu.CompilerParams(
            dimension_semantics=("arbitrary", "arbitrary", "arbitrary"), vmem_limit_bytes=VMEM_LIMIT),
        name="stickbreak_attn",
    )(q, kt, v, tmat, ga)


def _mix_kernel(yc_ref, ya_ref, x_ref, wo_ref, g2_ref, wrh_ref, wrl_ref, br_ref, tri_ref,
                x1_ref, xn_ref, meta_ref, cnt_ref, *, tm, width):
    i = pl.program_id(0)
    mix = jnp.dot(yc_ref[...], wo_ref[0:width, :], preferred_element_type=F32)
    mix = mix + jnp.dot(ya_ref[...], wo_ref[width:2 * width, :], preferred_element_type=F32)
    x1 = x_ref[...] + mix
    x1_ref[...] = x1
    ms = jnp.mean(x1 * x1, axis=-1, keepdims=True)
    xn = x1 * lax.rsqrt(ms + EPS) * g2_ref[...]
    xn_ref[...] = xn.reshape(tm, ROW_TILE, LANES)
    xh = xn.astype(BF16)
    xl = (xn - xh.astype(F32)).astype(BF16)
    logits = (jnp.dot(xh, wrh_ref[...], preferred_element_type=F32)
              + jnp.dot(xl, wrh_ref[...], preferred_element_type=F32)
              + jnp.dot(xh, wrl_ref[...], preferred_element_type=F32)) + br_ref[...]
    el = logits[:, :LANES]
    gl = logits[:, LANES:]
    lane = lax.broadcasted_iota(jnp.int32, (tm, LANES), 1)
    neg = jnp.float32(-jnp.inf)

    def first_argmax(vals):
        m = jnp.max(vals, axis=-1, keepdims=True)
        idx = jnp.min(jnp.where(vals == m, lane, LANES), axis=-1, keepdims=True)
        return m, idx

    glm = jnp.where(lane < N_GROUPS, gl, neg)
    gmax, gidx = first_argmax(glm)
    p_g = 1.0 / jnp.sum(jnp.exp(glm - gmax), axis=-1, keepdims=True)
    elm = jnp.where((lane // EXPERTS_PER_GROUP) == gidx, el, neg)
    v1, i1 = first_argmax(elm)
    v2, i2 = first_argmax(jnp.where(lane == i1, neg, elm))
    w = jnp.exp(v2 - v1)
    gate1 = p_g / (1.0 + w)
    gate2 = p_g * w / (1.0 + w)

    @pl.when(i == 0)
    def _():
        cnt_ref[...] = jnp.zeros_like(cnt_ref)

    o1 = lane == i1
    o2 = lane == i2
    osum = jnp.where(jnp.logical_or(o1, o2), 1.0, 0.0)
    base = jnp.dot(tri_ref[...], osum.astype(BF16), preferred_element_type=F32) + cnt_ref[...]
    rank1 = jnp.sum(jnp.where(o1, base, 0.0), axis=-1, keepdims=True)
    rank2 = jnp.sum(jnp.where(o2, base, 0.0), axis=-1, keepdims=True)
    cnt_ref[...] += jnp.sum(osum, axis=0, keepdims=True)

    cols = (i1.astype(F32), i2.astype(F32), rank1, rank2, gate1, gate2)
    meta = jnp.zeros((tm, LANES), F32)
    for n, cval in enumerate(cols):
        meta = jnp.where(lane == n, cval, meta)
    meta_ref[...] = meta


def _mix_call(yc, ya, x2d, w_out, g2, wr_hi, wr_lo, br, tri, *, tm):
    N, D = x2d.shape
    width = yc.shape[1]
    kern = functools.partial(_mix_kernel, tm=tm, width=width)
    row = lambda i: (i, 0)
    const = lambda i: (0, 0)
    return pl.pallas_call(
        kern,
        grid=(N // tm,),
        in_specs=[
            pl.BlockSpec((tm, width), row),
            pl.BlockSpec((tm, width), row),
            pl.BlockSpec((tm, D), row),
            pl.BlockSpec(w_out.shape, const),
            pl.BlockSpec((1, D), const),
            pl.BlockSpec(wr_hi.shape, const),
            pl.BlockSpec(wr_lo.shape, const),
            pl.BlockSpec(br.shape, const),
            pl.BlockSpec(tri.shape, const),
        ],
        out_specs=[
            pl.BlockSpec((tm, D), row),
            pl.BlockSpec((tm, ROW_TILE, LANES), lambda i: (i, 0, 0)),
            pl.BlockSpec((tm, LANES), row),
            pl.BlockSpec((1, LANES), const),
        ],
        out_shape=[
            jax.ShapeDtypeStruct((N, D), F32),
            jax.ShapeDtypeStruct((N, ROW_TILE, LANES), F32),
            jax.ShapeDtypeStruct((N, LANES), F32),
            jax.ShapeDtypeStruct((1, LANES), F32),
        ],
        compiler_params=pltpu.CompilerParams(
            dimension_semantics=("arbitrary",), vmem_limit_bytes=VMEM_LIMIT),
        name="outproj_router",
    )(yc, ya, x2d, w_out, g2, wr_hi, wr_lo, br, tri)


def _dispatch_kernel(pstart_ref, cnt_ref, pend_ref, dest_ref, xn_ref, xin_ref, zrow, sem, zsem, *, td):
    @pl.when(pl.program_id(0) == 0)
    def _():
        zrow[...] = jnp.zeros_like(zrow)

        def pad_rows(fn):
            def per_expert(e, _):
                lax.fori_loop(pstart_ref[e] + cnt_ref[e], pend_ref[e], fn, 0)
                return 0
            lax.fori_loop(0, N_EXPERTS, per_expert, 0)
            lax.fori_loop(pend_ref[N_EXPERTS - 1], xin_ref.shape[0], fn, 0)

        def start(p, _):
            pltpu.make_async_copy(zrow, xin_ref.at[p], zsem).start()
            return 0

        def wait(p, _):
            pltpu.make_async_copy(zrow, xin_ref.at[p], zsem).wait()
            return 0

        pad_rows(start)
        pad_rows(wait)

    def row_copy(r, k):
        return pltpu.make_async_copy(xn_ref.at[r], xin_ref.at[dest_ref[TOP_K * r + k]], sem)

    def start_rows(r, _):
        for k in range(TOP_K):
            row_copy(r, k).start()
        return 0

    def wait_rows(r, _):
        for k in range(TOP_K):
            row_copy(r, k).wait()
        return 0

    lax.fori_loop(0, td, start_rows, 0, unroll=DMA_UNROLL)
    lax.fori_loop(0, td, wait_rows, 0, unroll=DMA_UNROLL)


def _dispatch_call(pstart, cnt, pend, dest_flat, xn, n_rows, *, td):
    N = xn.shape[0]
    kern = functools.partial(_dispatch_kernel, td=td)
    smem_block = pl.BlockSpec((TOP_K * td,), lambda i, *_: (i,), memory_space=pltpu.SMEM)
    grid_spec = pltpu.PrefetchScalarGridSpec(
        num_scalar_prefetch=3,
        grid=(N // td,),
        in_specs=[smem_block, pl.BlockSpec((td, ROW_TILE, LANES), lambda i, *_: (i, 0, 0))],
        out_specs=pl.BlockSpec(memory_space=pl.ANY),
        scratch_shapes=[pltpu.VMEM((ROW_TILE, LANES), xn.dtype), pltpu.SemaphoreType.DMA(()),
                        pltpu.SemaphoreType.DMA(())],
    )
    return pl.pallas_call(
        kern,
        grid_spec=grid_spec,
        out_shape=jax.ShapeDtypeStruct((n_rows,) + xn.shape[1:], xn.dtype),
        compiler_params=pltpu.CompilerParams(dimension_semantics=("arbitrary",)),
        name="moe_dispatch",
    )(pstart, cnt, pend, dest_flat, xn)


def _ffn_kernel(blk_e_ref, n_used_ref, x_ref, w1_ref, w3_ref, w2_ref, y_ref, w1b, w3b, w2b):
    i = pl.program_id(0)
    prev = blk_e_ref[jnp.maximum(i - 1, 0)]
    changed = jnp.logical_or(i == 0, blk_e_ref[i] != prev)

    @pl.when(changed)
    def _():
        w1b[...] = w1_ref[...].astype(BF16)
        w3b[...] = w3_ref[...].astype(BF16)
        w2b[...] = w2_ref[...].astype(BF16)

    @pl.when(i < n_used_ref[0])
    def _():
        x = x_ref[...].reshape(DISPATCH_BLOCK, ROW_TILE * LANES).astype(BF16)
        a = jnp.dot(x, w1b[...], preferred_element_type=F32)
        b = jnp.dot(x, w3b[...], preferred_element_type=F32)
        h = (a * jax.nn.sigmoid(a) * b).astype(BF16)
        y = jnp.dot(h, w2b[...], preferred_element_type=F32)
        y_ref[...] = y.reshape(DISPATCH_BLOCK, ROW_TILE, LANES)

    @pl.when(i >= n_used_ref[0])
    def _():
        y_ref[...] = jnp.zeros_like(y_ref)


def _ffn_call(blk_e, n_used, xin, w1, w3, w2):
    P = xin.shape[0]
    E, D, DE = w1.shape
    n_blocks = P // DISPATCH_BLOCK
    tok_block = (DISPATCH_BLOCK, ROW_TILE, LANES)
    grid_spec = pltpu.PrefetchScalarGridSpec(
        num_scalar_prefetch=2,
        grid=(n_blocks,),
        in_specs=[
            pl.BlockSpec(tok_block, lambda i, be, nu: (jnp.minimum(i, nu[0] - 1), 0, 0)),
            pl.BlockSpec((None, D, DE), lambda i, be, nu: (be[i], 0, 0)),
            pl.BlockSpec((None, D, DE), lambda i, be, nu: (be[i], 0, 0)),
            pl.BlockSpec((None, DE, D), lambda i, be, nu: (be[i], 0, 0)),
        ],
        out_specs=pl.BlockSpec(tok_block, lambda i, be, nu: (i, 0, 0)),
        scratch_shapes=[pltpu.VMEM((D, DE), BF16), pltpu.VMEM((D, DE), BF16), pltpu.VMEM((DE, D), BF16)],
    )
    return pl.pallas_call(
        _ffn_kernel,
        grid_spec=grid_spec,
        out_shape=jax.ShapeDtypeStruct(xin.shape, F32),
        compiler_params=pltpu.CompilerParams(
            dimension_semantics=("arbitrary",), vmem_limit_bytes=VMEM_LIMIT),
        name="moe_expert_ffn",
    )(blk_e, n_used, xin, w1, w3, w2)


def _combine_kernel(dest_ref, dest_next_ref, y_ref, x1_ref, meta_ref, o_ref, ybuf, sems, *, tc):
    i = pl.program_id(0)
    slot = i % 2

    def gather(d_ref, s):
        def start(r, _):
            for k in range(TOP_K):
                pltpu.make_async_copy(y_ref.at[d_ref[TOP_K * r + k]], ybuf.at[s, k, r], sems.at[s]).start()
            return 0
        lax.fori_loop(0, tc, start, 0, unroll=DMA_UNROLL)

    @pl.when(i == 0)
    def _():
        gather(dest_ref, 0)

    @pl.when(i + 1 < pl.num_programs(0))
    def _():
        gather(dest_next_ref, 1 - slot)

    def wait(r, _):
        for k in range(TOP_K):
            pltpu.make_async_copy(y_ref.at[0], ybuf.at[slot, k, r], sems.at[slot]).wait()
        return 0

    lax.fori_loop(0, tc, wait, 0, unroll=DMA_UNROLL)
    meta = meta_ref[...]
    y1 = ybuf[slot, 0].reshape(tc, ROW_TILE * LANES)
    y2 = ybuf[slot, 1].reshape(tc, ROW_TILE * LANES)
    o_ref[...] = x1_ref[...] + (y1 * meta[:, 4:5] + y2 * meta[:, 5:6])


def _combine_call(dest_flat, y, x1, meta, *, tc):
    N, D = x1.shape
    n_tiles = N // tc
    kern = functools.partial(_combine_kernel, tc=tc)
    return pl.pallas_call(
        kern,
        grid=(n_tiles,),
        in_specs=[
            pl.BlockSpec((TOP_K * tc,), lambda i: (i,), memory_space=pltpu.SMEM),
            pl.BlockSpec((TOP_K * tc,), lambda i: (jnp.minimum(i + 1, n_tiles - 1),), memory_space=pltpu.SMEM),
            pl.BlockSpec(memory_space=pl.ANY),
            pl.BlockSpec((tc, D), lambda i: (i, 0)),
            pl.BlockSpec((tc, LANES), lambda i: (i, 0)),
        ],
        out_specs=pl.BlockSpec((tc, D), lambda i: (i, 0)),
        out_shape=jax.ShapeDtypeStruct((N, D), F32),
        scratch_shapes=[pltpu.VMEM((2, TOP_K, tc, ROW_TILE, LANES), F32), pltpu.SemaphoreType.DMA((2,))],
        compiler_params=pltpu.CompilerParams(
            dimension_semantics=("arbitrary",), vmem_limit_bytes=VMEM_LIMIT),
        name="moe_combine",
    )(dest_flat, dest_flat, y, x1, meta)


def _suffix_matrix():
    r = np.arange(KEY_BLOCK)[:, None]
    c = np.arange(KEY_BLOCK)[None, :]
    return jnp.asarray(r >= c, BF16)


def _layer(x, norm_mix_w, w_in, conv_w, q_norm_w, k_norm_w, gn_conv_w, gn_attn_w, w_out,
           norm_ffn_w, w_group, b_group, w_expert, b_expert, w1, w3, w2):
    B, S, D = x.shape
    N = B * S
    ts = min(512, S)
    tq = min(2 * KEY_BLOCK, S)
    tm = min(512, N)
    scale = 1.0 / float(np.sqrt(HEAD_DIM))

    gq = (jnp.tile(q_norm_w, N_HEADS) * scale)[None, :].astype(F32)
    gk = jnp.tile(k_norm_w, N_HEADS)[None, :].astype(F32)
    gc = gn_conv_w.reshape(1, -1).astype(F32)
    ga = gn_attn_w.reshape(1, -1).astype(F32)
    yc, q, kt, v = _proj_call(x, norm_mix_w[None, :], w_in.astype(BF16), conv_w, gq, gk, gc, ts=ts)
    ya = _attn_call(q, kt, v, _suffix_matrix(), ga, tq=tq)

    pad_lanes = lambda a: jnp.pad(a.astype(F32), ((0, 0), (0, LANES - a.shape[1])))
    wr = jnp.concatenate([pad_lanes(w_expert), pad_lanes(w_group)], axis=1)
    br = jnp.concatenate([pad_lanes(b_expert[None, :]), pad_lanes(b_group[None, :])], axis=1)
    wr_hi = wr.astype(BF16)
    wr_lo = (wr - wr_hi.astype(F32)).astype(BF16)
    tri = jnp.asarray(np.tril(np.ones((tm, tm), np.float32), -1), BF16)
    x1, xn2, meta, counts = _mix_call(
        yc.reshape(N, -1), ya.reshape(N, -1), x.reshape(N, D), w_out.astype(BF16),
        norm_ffn_w[None, :], wr_hi, wr_lo, br, tri, tm=tm)

    eid = meta[:, 0:TOP_K].astype(jnp.int32).reshape(-1)
    rank = meta[:, TOP_K:2 * TOP_K].astype(jnp.int32).reshape(-1)
    cnt = counts[0, :N_EXPERTS].astype(jnp.int32)
    pcnt = (cnt + DISPATCH_BLOCK - 1) // DISPATCH_BLOCK * DISPATCH_BLOCK
    pend = jnp.cumsum(pcnt)
    pstart = pend - pcnt
    n_blocks = -(-N * TOP_K // DISPATCH_BLOCK) + N_EXPERTS
    blk_start = jnp.arange(n_blocks, dtype=jnp.int32) * DISPATCH_BLOCK
    blk_e = jnp.sum((blk_start[:, None] >= pend[None, :]).astype(jnp.int32), axis=1)
    blk_e = jnp.minimum(blk_e, N_EXPERTS - 1)
    n_used = pend[-1:] // DISPATCH_BLOCK
    onehot = eid[:, None] == jnp.arange(N_EXPERTS, dtype=jnp.int32)[None, :]
    dest = rank + jnp.sum(jnp.where(onehot, pstart[None, :], 0), axis=1)

    xin = _dispatch_call(pstart, cnt, pend, dest, xn2, n_blocks * DISPATCH_BLOCK, td=min(512, N))
    y = _ffn_call(blk_e, n_used, xin, w1, w3, w2)
    out = _combine_call(dest, y, x1, meta, tc=min(256, N))
    return out.reshape(B, S, D)


def kernel(x, norm_mix_w, w_in, conv_w, q_norm_w, k_norm_w, group_norm_conv_w, group_norm_attn_w, w_out,
           norm_ffn_w, w_group, b_group, w_expert, b_expert, w1, w3, w2):
    depth = norm_mix_w.shape[0]
    for l in range(depth):
        x = _layer(x, norm_mix_w[l], w_in[l], conv_w[l], q_norm_w[l], k_norm_w[l], group_norm_conv_w[l],
                   group_norm_attn_w[l], w_out[l], norm_ffn_w[l], w_group[l], b_group[l], w_expert[l],
                   b_expert[l], w1[l], w3[l], w2[l])
    return x
```

```python
import functools

import numpy as np
import jax
import jax.numpy as jnp
from jax import lax
from jax.experimental import pallas as pl
from jax.experimental.pallas import tpu as pltpu

F32 = jnp.float32
BF16 = jnp.bfloat16

EPS = 1e-6
MASKED = -1e30
STICK_GONE = 110.0
LANES = 128
HEAD_DIM = 64
N_HEADS = 8
N_CONV_GROUPS = 8
CONV_K = 3
N_GROUPS = 4
EXPERTS_PER_GROUP = 8
N_EXPERTS = N_GROUPS * EXPERTS_PER_GROUP
TOP_K = 2
DISPATCH_BLOCK = 512
HEAD_PAIRS = N_HEADS * HEAD_DIM // LANES

ROW_TILE = 8
KEY_BLOCK = 256
DMA_UNROLL = 4
VMEM_LIMIT = 56 * 1024 * 1024


def _pair_group_rms(x, lane_lo):
    x2 = x * x
    s_lo = jnp.sum(jnp.where(lane_lo, x2, 0.0), axis=-1, keepdims=True)
    s_hi = jnp.sum(jnp.where(lane_lo, 0.0, x2), axis=-1, keepdims=True)
    ms = jnp.where(lane_lo, s_lo, s_hi) * (1.0 / HEAD_DIM)
    return x * lax.rsqrt(ms + EPS)


def _proj_kernel(x_ref, g1_ref, win_ref, cw_ref, gq_ref, gk_ref, gc_ref,
                 yc_ref, q_ref, kt_ref, v_ref, ubuf, *, ts, width):
    i = pl.program_id(1)
    x = x_ref[0]
    ms = jnp.mean(x * x, axis=-1, keepdims=True)
    xn = (x * lax.rsqrt(ms + EPS) * g1_ref[...]).astype(BF16)

    def proj(c):
        return jnp.dot(xn, win_ref[:, c * width:(c + 1) * width], preferred_element_type=F32)

    lane_lo = lax.broadcasted_iota(jnp.int32, (ts, LANES), 1) < HEAD_DIM

    u = proj(1) * proj(2)

    @pl.when(i == 0)
    def _():
        ubuf[0:8, :] = jnp.zeros((8, width), F32)

    ubuf[8:8 + ts, :] = u
    u1 = ubuf[7:7 + ts, :]
    u2 = ubuf[6:6 + ts, :]
    ubuf[0:8, :] = u[ts - 8:ts, :]
    y = cw_ref[0:1, :] * u2 + cw_ref[1:2, :] * u1 + cw_ref[2:3, :] * u
    yc = proj(0) * y
    for c in range(width // LANES):
        sl = slice(c * LANES, (c + 1) * LANES)
        yc_ref[0, :, sl] = (_pair_group_rms(yc[:, sl], lane_lo) * gc_ref[:, sl]).astype(BF16)

    q = proj(3)
    for c in range(width // LANES):
        sl = slice(c * LANES, (c + 1) * LANES)
        q_ref[0, :, sl] = (_pair_group_rms(q[:, sl], lane_lo) * gq_ref[:, sl]).astype(BF16)

    k = proj(4)
    for c in range(width // LANES):
        sl = slice(c * LANES, (c + 1) * LANES)
        kn = _pair_group_rms(k[:, sl], lane_lo) * gk_ref[:, sl]
        knt = kn.T.astype(BF16)
        for j in range(ts // KEY_BLOCK):
            kt_ref[0, c, j] = knt[:, j * KEY_BLOCK:(j + 1) * KEY_BLOCK]

    v_ref[0] = proj(5).astype(BF16)


def _proj_call(x, g1, w_in, conv_w, gq, gk, gc, *, ts):
    B, S, D = x.shape
    width = w_in.shape[1] // 6
    nkb = S // KEY_BLOCK
    kern = functools.partial(_proj_kernel, ts=ts, width=width)
    const = lambda b, i: (0, 0)
    return pl.pallas_call(
        kern,
        grid=(B, S // ts),
        in_specs=[
            pl.BlockSpec((1, ts, D), lambda b, i: (b, i, 0)),
            pl.BlockSpec((1, D), const),
            pl.BlockSpec(w_in.shape, const),
            pl.BlockSpec(conv_w.shape, const),
            pl.BlockSpec((1, width), const),
            pl.BlockSpec((1, width), const),
            pl.BlockSpec((1, width), const),
        ],
        out_specs=[
            pl.BlockSpec((1, ts, width), lambda b, i: (b, i, 0)),
            pl.BlockSpec((1, ts, width), lambda b, i: (b, i, 0)),
            pl.BlockSpec((1, HEAD_PAIRS, ts // KEY_BLOCK, LANES, KEY_BLOCK), lambda b, i: (b, 0, i, 0, 0)),
            pl.BlockSpec((1, ts, width), lambda b, i: (b, i, 0)),
        ],
        out_shape=[
            jax.ShapeDtypeStruct((B, S, width), BF16),
            jax.ShapeDtypeStruct((B, S, width), BF16),
            jax.ShapeDtypeStruct((B, HEAD_PAIRS, nkb, LANES, KEY_BLOCK), BF16),
            jax.ShapeDtypeStruct((B, S, width), BF16),
        ],
        scratch_shapes=[pltpu.VMEM((ts + 8, width), F32)],
        compiler_params=pltpu.CompilerParams(
            dimension_semantics=("arbitrary", "arbitrary"), vmem_limit_bytes=VMEM_LIMIT),
        name="proj_conv_qknorm",
    )(x, g1, w_in, conv_w, gq, gk, gc)


def _attn_kernel(q_ref, kt_ref, v_ref, tm_ref, ga_ref, o_ref, acc_ref, carry_ref, left_ref, *, tq):
    qi = pl.program_id(2)
    lane_lo = lax.broadcasted_iota(jnp.int32, (tq, LANES), 1) < HEAD_DIM
    q = q_ref[0]
    zero = jnp.zeros_like(q)
    qh = (jnp.where(lane_lo, q, zero), jnp.where(lane_lo, zero, q))
    acc_ref[...] = jnp.zeros_like(acc_ref)
    carry_ref[...] = jnp.zeros_like(carry_ref)

    n_sub = tq // KEY_BLOCK

    def key_block(j, s, diagonal):
        rows = slice(s * KEY_BLOCK, (s + 1) * KEY_BLOCK)
        kt = kt_ref[0, 0, j]
        start = pl.multiple_of(j * KEY_BLOCK, KEY_BLOCK)
        vb = v_ref[0, pl.ds(start, KEY_BLOCK), :]
        for h in range(2):
            z = jnp.dot(qh[h][rows, :], kt, preferred_element_type=F32)
            sp = jnp.maximum(jnp.log(1.0 + jnp.exp(jnp.minimum(z, 80.0))), z)
            if diagonal:
                row = lax.broadcasted_iota(jnp.int32, (KEY_BLOCK, KEY_BLOCK), 0)
                col = lax.broadcasted_iota(jnp.int32, (KEY_BLOCK, KEY_BLOCK), 1)
                sp = jnp.where(col < row, sp, 0.0)
                z = jnp.where(col < row, z, MASKED)
            r = jnp.dot(sp.astype(BF16), tm_ref[...], preferred_element_type=F32)
            carry = carry_ref[h, rows, :]
            a = jnp.exp(z - (r + jnp.concatenate([carry] * (KEY_BLOCK // LANES), axis=1)))
            carry = carry + jnp.broadcast_to(r[:, 0:1], carry.shape)
            carry_ref[h, rows, :] = carry
            left_ref[h * n_sub + s:h * n_sub + s + 1, :] = jnp.min(carry, axis=0, keepdims=True)
            acc_ref[h, rows, :] += jnp.dot(a.astype(BF16), vb, preferred_element_type=F32)

    first = qi * n_sub

    @pl.when(qi == 0)
    def _():
        key_block(0, 0, True)
        for s in range(1, n_sub):
            key_block(s, s, True)
            key_block(s - 1, s, False)

    @pl.when(qi > 0)
    def _():
        for s in range(n_sub):
            key_block(first + s, s, True)
            key_block(first + s - 1, s, False)

    def stick_left(t):
        return jnp.logical_and(first + n_sub - 3 - t >= 0, jnp.min(left_ref[...]) < STICK_GONE)

    def step(t):
        for s in range(n_sub):
            j = first + s - 2 - t

            @pl.when(j >= 0)
            def _():
                key_block(j, s, False)
        return t + 1

    lax.while_loop(stick_left, step, 0)

    o = jnp.where(lane_lo, acc_ref[0], acc_ref[1])
    o_ref[0] = (_pair_group_rms(o, lane_lo) * ga_ref[...]).astype(BF16)


def _attn_call(q, kt, v, tmat, ga, *, tq):
    B, S, width = q.shape
    assert tq % KEY_BLOCK == 0 and S % tq == 0
    kern = functools.partial(_attn_kernel, tq=tq)
    return pl.pallas_call(
        kern,
        grid=(B, HEAD_PAIRS, S // tq),
        in_specs=[
            pl.BlockSpec((1, tq, LANES), lambda b, p, i: (b, i, p)),
            pl.BlockSpec((1, 1) + kt.shape[2:], lambda b, p, i: (b, p, 0, 0, 0)),
            pl.BlockSpec((1, S, LANES), lambda b, p, i: (b, 0, p)),
            pl.BlockSpec(tmat.shape, lambda b, p, i: (0, 0)),
            pl.BlockSpec((1, LANES), lambda b, p, i: (0, p)),
        ],
        out_specs=pl.BlockSpec((1, tq, LANES), lambda b, p, i: (b, i, p)),
        out_shape=jax.ShapeDtypeStruct((B, S, width), BF16),
        scratch_shapes=[pltpu.VMEM((2, tq, LANES), F32), pltpu.VMEM((2, tq, LANES), F32),
                        pltpu.VMEM((2 * (tq // KEY_BLOCK), LANES), F32)],
        compiler_params=pltpu.CompilerParams(
            dimension_semantics=("arbitrary", "arbitrary", "arbitrary"), vmem_limit_bytes=VMEM_LIMIT),
        name="stickbreak_attn",
    )(q, kt, v, tmat, ga)


def _mix_kernel(yc_ref, ya_ref, x_ref, wo_ref, g2_ref, wrh_ref, wrl_ref, br_ref, tri_ref,
                x1_ref, xn_ref, meta_ref, cnt_ref, *, tm, width):
    i = pl.program_id(0)
    mix = jnp.dot(yc_ref[...], wo_ref[0:width, :], preferred_element_type=F32)
    mix = mix + jnp.dot(ya_ref[...], wo_ref[width:2 * width, :], preferred_element_type=F32)
    x1 = x_ref[...] + mix
    x1_ref[...] = x1
    ms = jnp.mean(x1 * x1, axis=-1, keepdims=True)
    xn = x1 * lax.rsqrt(ms + EPS) * g2_ref[...]
    xn_ref[...] = xn.reshape(tm, ROW_TILE, LANES)
    xh = xn.astype(BF16)
    xl = (xn - xh.astype(F32)).astype(BF16)
    logits = (jnp.dot(xh, wrh_ref[...], preferred_element_type=F32)
              + jnp.dot(xl, wrh_ref[...], preferred_element_type=F32)
              + jnp.dot(xh, wrl_ref[...], preferred_element_type=F32)) + br_ref[...]
    el = logits[:, :LANES]
    gl = logits[:, LANES:]
    lane = lax.broadcasted_iota(jnp.int32, (tm, LANES), 1)
    neg = jnp.float32(-jnp.inf)

    def first_argmax(vals):
        m = jnp.max(vals, axis=-1, keepdims=True)
        idx = jnp.min(jnp.where(vals == m, lane, LANES), axis=-1, keepdims=True)
        return m, idx

    glm = jnp.where(lane < N_GROUPS, gl, neg)
    gmax, gidx = first_argmax(glm)
    p_g = 1.0 / jnp.sum(jnp.exp(glm - gmax), axis=-1, keepdims=True)
    elm = jnp.where((lane // EXPERTS_PER_GROUP) == gidx, el, neg)
    v1, i1 = first_argmax(elm)
    v2, i2 = first_argmax(jnp.where(lane == i1, neg, elm))
    w = jnp.exp(v2 - v1)
    gate1 = p_g / (1.0 + w)
    gate2 = p_g * w / (1.0 + w)

    @pl.when(i == 0)
    def _():
        cnt_ref[...] = jnp.zeros_like(cnt_ref)

    o1 = lane == i1
    o2 = lane == i2
    osum = jnp.where(jnp.logical_or(o1, o2), 1.0, 0.0)
    base = jnp.dot(tri_ref[...], osum.astype(BF16), preferred_element_type=F32) + cnt_ref[...]
    rank1 = jnp.sum(jnp.where(o1, base, 0.0), axis=-1, keepdims=True)
    rank2 = jnp.sum(jnp.where(o2, base, 0.0), axis=-1, keepdims=True)
    cnt_ref[...] += jnp.sum(osum, axis=0, keepdims=True)

    cols = (i1.astype(F32), i2.astype(F32), rank1, rank2, gate1, gate2)
    meta = jnp.zeros((tm, LANES), F32)
    for n, cval in enumerate(cols):
        meta = jnp.where(lane == n, cval, meta)
    meta_ref[...] = meta


def _mix_call(yc, ya, x2d, w_out, g2, wr_hi, wr_lo, br, tri, *, tm):
    N, D = x2d.shape
    width = yc.shape[1]
    kern = functools.partial(_mix_kernel, tm=tm, width=width)
    row = lambda i: (i, 0)
    const = lambda i: (0, 0)
    return pl.pallas_call(
        kern,
        grid=(N // tm,),
        in_specs=[
            pl.BlockSpec((tm, width), row),
            pl.BlockSpec((tm, width), row),
            pl.BlockSpec((tm, D), row),
            pl.BlockSpec(w_out.shape, const),
            pl.BlockSpec((1, D), const),
            pl.BlockSpec(wr_hi.shape, const),
            pl.BlockSpec(wr_lo.shape, const),
            pl.BlockSpec(br.shape, const),
            pl.BlockSpec(tri.shape, const),
        ],
        out_specs=[
            pl.BlockSpec((tm, D), row),
            pl.BlockSpec((tm, ROW_TILE, LANES), lambda i: (i, 0, 0)),
            pl.BlockSpec((tm, LANES), row),
            pl.BlockSpec((1, LANES), const),
        ],
        out_shape=[
            jax.ShapeDtypeStruct((N, D), F32),
            jax.ShapeDtypeStruct((N, ROW_TILE, LANES), F32),
            jax.ShapeDtypeStruct((N, LANES), F32),
            jax.ShapeDtypeStruct((1, LANES), F32),
        ],
        compiler_params=pltpu.CompilerParams(
            dimension_semantics=("arbitrary",), vmem_limit_bytes=VMEM_LIMIT),
        name="outproj_router",
    )(yc, ya, x2d, w_out, g2, wr_hi, wr_lo, br, tri)


def _dispatch_kernel(pstart_ref, cnt_ref, pend_ref, dest_ref, xn_ref, xin_ref, zblk, sem, zsem, *, td):
    del pstart_ref
    n_blocks = xin_ref.shape[0] // DISPATCH_BLOCK

    @pl.when(pl.program_id(0) == 0)
    def _():
        zblk[...] = jnp.zeros_like(zblk)

        def zero_block(b):
            start = pl.multiple_of(b * DISPATCH_BLOCK, DISPATCH_BLOCK)
            return pltpu.make_async_copy(zblk, xin_ref.at[pl.ds(start, DISPATCH_BLOCK)], zsem)

        def padded_blocks(fn):
            def per_expert(e, _):
                @pl.when(cnt_ref[e] % DISPATCH_BLOCK != 0)
                def _():
                    fn(zero_block(pend_ref[e] // DISPATCH_BLOCK - 1))
                return 0

            def unused(b, _):
                fn(zero_block(b))
                return 0

            lax.fori_loop(0, N_EXPERTS, per_expert, 0)
            lax.fori_loop(pend_ref[N_EXPERTS - 1] // DISPATCH_BLOCK, n_blocks, unused, 0)

        padded_blocks(lambda copy: copy.start())
        padded_blocks(lambda copy: copy.wait())

    def row_copy(r, k):
        return pltpu.make_async_copy(xn_ref.at[r], xin_ref.at[dest_ref[TOP_K * r + k]], sem)

    def start_rows(r, _):
        for k in range(TOP_K):
            row_copy(r, k).start()
        return 0

    def wait_rows(r, _):
        for k in range(TOP_K):
            row_copy(r, k).wait()
        return 0

    lax.fori_loop(0, td, start_rows, 0, unroll=DMA_UNROLL)
    lax.fori_loop(0, td, wait_rows, 0, unroll=DMA_UNROLL)


def _dispatch_call(pstart, cnt, pend, dest_flat, xn, n_rows, *, td):
    N = xn.shape[0]
    kern = functools.partial(_dispatch_kernel, td=td)
    smem_block = pl.BlockSpec((TOP_K * td,), lambda i, *_: (i,), memory_space=pltpu.SMEM)
    grid_spec = pltpu.PrefetchScalarGridSpec(
        num_scalar_prefetch=3,
        grid=(N // td,),
        in_specs=[smem_block, pl.BlockSpec((td, ROW_TILE, LANES), lambda i, *_: (i, 0, 0))],
        out_specs=pl.BlockSpec(memory_space=pl.ANY),
        scratch_shapes=[pltpu.VMEM((DISPATCH_BLOCK, ROW_TILE, LANES), xn.dtype), pltpu.SemaphoreType.DMA(()),
                        pltpu.SemaphoreType.DMA(())],
    )
    return pl.pallas_call(
        kern,
        grid_spec=grid_spec,
        out_shape=jax.ShapeDtypeStruct((n_rows,) + xn.shape[1:], xn.dtype),
        compiler_params=pltpu.CompilerParams(dimension_semantics=("arbitrary",)),
        name="moe_dispatch",
    )(pstart, cnt, pend, dest_flat, xn)


def _ffn_kernel(blk_e_ref, n_used_ref, x_ref, w1_ref, w3_ref, w2_ref, y_ref, w1b, w3b, w2b):
    i = pl.program_id(0)
    prev = blk_e_ref[jnp.maximum(i - 1, 0)]
    changed = jnp.logical_or(i == 0, blk_e_ref[i] != prev)

    @pl.when(changed)
    def _():
        w1b[...] = w1_ref[...].astype(BF16)
        w3b[...] = w3_ref[...].astype(BF16)
        w2b[...] = w2_ref[...].astype(BF16)

    @pl.when(i < n_used_ref[0])
    def _():
        x = x_ref[...].reshape(DISPATCH_BLOCK, ROW_TILE * LANES).astype(BF16)
        a = jnp.dot(x, w1b[...], preferred_element_type=F32)
        b = jnp.dot(x, w3b[...], preferred_element_type=F32)
        h = (a * jax.nn.sigmoid(a) * b).astype(BF16)
        y = jnp.dot(h, w2b[...], preferred_element_type=F32)
        y_ref[...] = y.reshape(DISPATCH_BLOCK, ROW_TILE, LANES)

    @pl.when(i >= n_used_ref[0])
    def _():
        y_ref[...] = jnp.zeros_like(y_ref)


def _ffn_call(blk_e, n_used, xin, w1, w3, w2):
    P = xin.shape[0]
    E, D, DE = w1.shape
    n_blocks = P // DISPATCH_BLOCK
    tok_block = (DISPATCH_BLOCK, ROW_TILE, LANES)
    grid_spec = pltpu.PrefetchScalarGridSpec(
        num_scalar_prefetch=2,
        grid=(n_blocks,),
        in_specs=[
            pl.BlockSpec(tok_block, lambda i, be, nu: (jnp.minimum(i, nu[0] - 1), 0, 0)),
            pl.BlockSpec((None, D, DE), lambda i, be, nu: (be[i], 0, 0)),
            pl.BlockSpec((None, D, DE), lambda i, be, nu: (be[i], 0, 0)),
            pl.BlockSpec((None, DE, D), lambda i, be, nu: (be[i], 0, 0)),
        ],
        out_specs=pl.BlockSpec(tok_block, lambda i, be, nu: (i, 0, 0)),
        scratch_shapes=[pltpu.VMEM((D, DE), BF16), pltpu.VMEM((D, DE), BF16), pltpu.VMEM((DE, D), BF16)],
    )
    return pl.pallas_call(
        _ffn_kernel,
        grid_spec=grid_spec,
        out_shape=jax.ShapeDtypeStruct(xin.shape, F32),
        compiler_params=pltpu.CompilerParams(
            dimension_semantics=("arbitrary",), vmem_limit_bytes=VMEM_LIMIT),
        name="moe_expert_ffn",
    )(blk_e, n_used, xin, w1, w3, w2)


def _combine_kernel(dest_ref, dest_next_ref, y_ref, x1_ref, meta_ref, o_ref, ybuf, sems, *, tc):
    i = pl.program_id(0)
    slot = i % 2

    def gather(d_ref, s):
        def start(r, _):
            for k in range(TOP_K):
                pltpu.make_async_copy(y_ref.at[d_ref[TOP_K * r + k]], ybuf.at[s, k, r], sems.at[s]).start()
            return 0
        lax.fori_loop(0, tc, start, 0, unroll=DMA_UNROLL)

    @pl.when(i == 0)
    def _():
        gather(dest_ref, 0)

    @pl.when(i + 1 < pl.num_programs(0))
    def _():
        gather(dest_next_ref, 1 - slot)

    def wait(r, _):
        for k in range(TOP_K):
            pltpu.make_async_copy(y_ref.at[0], ybuf.at[slot, k, r], sems.at[slot]).wait()
        return 0

    lax.fori_loop(0, tc, wait, 0, unroll=DMA_UNROLL)
    meta = meta_ref[...]
    y1 = ybuf[slot, 0].reshape(tc, ROW_TILE * LANES)
    y2 = ybuf[slot, 1].reshape(tc, ROW_TILE * LANES)
    o_ref[...] = x1_ref[...] + (y1 * meta[:, 4:5] + y2 * meta[:, 5:6])


def _combine_call(dest_flat, y, x1, meta, *, tc):
    N, D = x1.shape
    n_tiles = N // tc
    kern = functools.partial(_combine_kernel, tc=tc)
    return pl.pallas_call(
        kern,
        grid=(n_tiles,),
        in_specs=[
            pl.BlockSpec((TOP_K * tc,), lambda i: (i,), memory_space=pltpu.SMEM),
            pl.BlockSpec((TOP_K * tc,), lambda i: (jnp.minimum(i + 1, n_tiles - 1),), memory_space=pltpu.SMEM),
            pl.BlockSpec(memory_space=pl.ANY),
            pl.BlockSpec((tc, D), lambda i: (i, 0)),
            pl.BlockSpec((tc, LANES), lambda i: (i, 0)),
        ],
        out_specs=pl.BlockSpec((tc, D), lambda i: (i, 0)),
        out_shape=jax.ShapeDtypeStruct((N, D), F32),
        scratch_shapes=[pltpu.VMEM((2, TOP_K, tc, ROW_TILE, LANES), F32), pltpu.SemaphoreType.DMA((2,))],
        compiler_params=pltpu.CompilerParams(
            dimension_semantics=("arbitrary",), vmem_limit_bytes=VMEM_LIMIT),
        name="moe_combine",
    )(dest_flat, dest_flat, y, x1, meta)


def _suffix_matrix():
    r = np.arange(KEY_BLOCK)[:, None]
    c = np.arange(KEY_BLOCK)[None, :]
    return jnp.asarray(r >= c, BF16)


def _layer(x, norm_mix_w, w_in, conv_w, q_norm_w, k_norm_w, gn_conv_w, gn_attn_w, w_out,
           norm_ffn_w, w_group, b_group, w_expert, b_expert, w1, w3, w2):
    B, S, D = x.shape
    N = B * S
    ts = min(512, S)
    tq = min(4 * KEY_BLOCK, S)
    tm = min(512, N)
    scale = 1.0 / float(np.sqrt(HEAD_DIM))

    gq = (jnp.tile(q_norm_w, N_HEADS) * scale)[None, :].astype(F32)
    gk = jnp.tile(k_norm_w, N_HEADS)[None, :].astype(F32)
    gc = gn_conv_w.reshape(1, -1).astype(F32)
    ga = gn_attn_w.reshape(1, -1).astype(F32)
    yc, q, kt, v = _proj_call(x, norm_mix_w[None, :], w_in.astype(BF16), conv_w, gq, gk, gc, ts=ts)
    ya = _attn_call(q, kt, v, _suffix_matrix(), ga, tq=tq)

    pad_lanes = lambda a: jnp.pad(a.astype(F32), ((0, 0), (0, LANES - a.shape[1])))
    wr = jnp.concatenate([pad_lanes(w_expert), pad_lanes(w_group)], axis=1)
    br = jnp.concatenate([pad_lanes(b_expert[None, :]), pad_lanes(b_group[None, :])], axis=1)
    wr_hi = wr.astype(BF16)
    wr_lo = (wr - wr_hi.astype(F32)).astype(BF16)
    tri = jnp.asarray(np.tril(np.ones((tm, tm), np.float32), -1), BF16)
    x1, xn2, meta, counts = _mix_call(
        yc.reshape(N, -1), ya.reshape(N, -1), x.reshape(N, D), w_out.astype(BF16),
        norm_ffn_w[None, :], wr_hi, wr_lo, br, tri, tm=tm)

    eid = meta[:, 0:TOP_K].astype(jnp.int32).reshape(-1)
    rank = meta[:, TOP_K:2 * TOP_K].astype(jnp.int32).reshape(-1)
    cnt = counts[0, :N_EXPERTS].astype(jnp.int32)
    pcnt = (cnt + DISPATCH_BLOCK - 1) // DISPATCH_BLOCK * DISPATCH_BLOCK
    pend = jnp.cumsum(pcnt)
    pstart = pend - pcnt
    n_blocks = -(-N * TOP_K // DISPATCH_BLOCK) + N_EXPERTS
    blk_start = jnp.arange(n_blocks, dtype=jnp.int32) * DISPATCH_BLOCK
    blk_e = jnp.sum((blk_start[:, None] >= pend[None, :]).astype(jnp.int32), axis=1)
    blk_e = jnp.minimum(blk_e, N_EXPERTS - 1)
    n_used = pend[-1:] // DISPATCH_BLOCK
    onehot = eid[:, None] == jnp.arange(N_EXPERTS, dtype=jnp.int32)[None, :]
    dest = rank + jnp.sum(jnp.where(onehot, pstart[None, :], 0), axis=1)

    xin = _dispatch_call(pstart, cnt, pend, dest, xn2, n_blocks * DISPATCH_BLOCK, td=min(512, N))
    y = _ffn_call(blk_e, n_used, xin, w1, w3, w2)
    out = _combine_call(dest, y, x1, meta, tc=min(256, N))
    return out.reshape(B, S, D)


def kernel(x, norm_mix_w, w_in, conv_w, q_norm_w, k_norm_w, group_norm_conv_w, group_norm_attn_w, w_out,
           norm_ffn_w, w_group, b_group, w_expert, b_expert, w1, w3, w2):
    depth = norm_mix_w.shape[0]
    for l in range(depth):
        x = _layer(x, norm_mix_w[l], w_in[l], conv_w[l], q_norm_w[l], k_norm_w[l], group_norm_conv_w[l],
                   group_norm_attn_w[l], w_out[l], norm_ffn_w[l], w_group[l], b_group[l], w_expert[l],
                   b_expert[l], w1[l], w3[l], w2[l])
    return x
```

```python
import functools

import numpy as np
import jax
import jax.numpy as jnp
from jax import lax
from jax.experimental import pallas as pl
from jax.experimental.pallas import tpu as pltpu

F32 = jnp.float32
BF16 = jnp.bfloat16

EPS = 1e-6
MASKED = -1e30
STICK_GONE = 110.0
LANES = 128
HEAD_DIM = 64
N_HEADS = 8
N_CONV_GROUPS = 8
CONV_K = 3
N_GROUPS = 4
EXPERTS_PER_GROUP = 8
N_EXPERTS = N_GROUPS * EXPERTS_PER_GROUP
TOP_K = 2
DISPATCH_BLOCK = 512
HEAD_PAIRS = N_HEADS * HEAD_DIM // LANES

ROW_TILE = 8
KEY_BLOCK = 256
DMA_UNROLL = 4
N_DMA_QUEUES = 2
ROUTE_CHUNK = 512
VMEM_LIMIT = 56 * 1024 * 1024


def _pair_group_rms(x, lane_lo):
    x2 = x * x
    s_lo = jnp.sum(jnp.where(lane_lo, x2, 0.0), axis=-1, keepdims=True)
    s_hi = jnp.sum(jnp.where(lane_lo, 0.0, x2), axis=-1, keepdims=True)
    ms = jnp.where(lane_lo, s_lo, s_hi) * (1.0 / HEAD_DIM)
    return x * lax.rsqrt(ms + EPS)


def _proj_kernel(x_ref, g1_ref, win_ref, cw_ref, gq_ref, gk_ref, gc_ref,
                 yc_ref, q_ref, kt_ref, v_ref, ubuf, *, ts, width):
    i = pl.program_id(1)
    x = x_ref[0]
    ms = jnp.mean(x * x, axis=-1, keepdims=True)
    xn = (x * lax.rsqrt(ms + EPS) * g1_ref[...]).astype(BF16)

    def proj(c):
        return jnp.dot(xn, win_ref[:, c * width:(c + 1) * width], preferred_element_type=F32)

    lane_lo = lax.broadcasted_iota(jnp.int32, (ts, LANES), 1) < HEAD_DIM

    u = proj(1) * proj(2)

    @pl.when(i == 0)
    def _():
        ubuf[0:8, :] = jnp.zeros((8, width), F32)

    ubuf[8:8 + ts, :] = u
    u1 = ubuf[7:7 + ts, :]
    u2 = ubuf[6:6 + ts, :]
    ubuf[0:8, :] = u[ts - 8:ts, :]
    y = cw_ref[0:1, :] * u2 + cw_ref[1:2, :] * u1 + cw_ref[2:3, :] * u
    yc = proj(0) * y
    for c in range(width // LANES):
        sl = slice(c * LANES, (c + 1) * LANES)
        yc_ref[0, :, sl] = (_pair_group_rms(yc[:, sl], lane_lo) * gc_ref[:, sl]).astype(BF16)

    q = proj(3)
    for c in range(width // LANES):
        sl = slice(c * LANES, (c + 1) * LANES)
        q_ref[0, :, sl] = (_pair_group_rms(q[:, sl], lane_lo) * gq_ref[:, sl]).astype(BF16)

    k = proj(4)
    for c in range(width // LANES):
        sl = slice(c * LANES, (c + 1) * LANES)
        kn = _pair_group_rms(k[:, sl], lane_lo) * gk_ref[:, sl]
        knt = kn.T.astype(BF16)
        for j in range(ts // KEY_BLOCK):
            kt_ref[0, c, j] = knt[:, j * KEY_BLOCK:(j + 1) * KEY_BLOCK]

    v_ref[0] = proj(5).astype(BF16)


def _proj_call(x, g1, w_in, conv_w, gq, gk, gc, *, ts):
    B, S, D = x.shape
    width = w_in.shape[1] // 6
    nkb = S // KEY_BLOCK
    kern = functools.partial(_proj_kernel, ts=ts, width=width)
    const = lambda b, i: (0, 0)
    return pl.pallas_call(
        kern,
        grid=(B, S // ts),
        in_specs=[
            pl.BlockSpec((1, ts, D), lambda b, i: (b, i, 0)),
            pl.BlockSpec((1, D), const),
            pl.BlockSpec(w_in.shape, const),
            pl.BlockSpec(conv_w.shape, const),
            pl.BlockSpec((1, width), const),
            pl.BlockSpec((1, width), const),
            pl.BlockSpec((1, width), const),
        ],
        out_specs=[
            pl.BlockSpec((1, ts, width), lambda b, i: (b, i, 0)),
            pl.BlockSpec((1, ts, width), lambda b, i: (b, i, 0)),
            pl.BlockSpec((1, HEAD_PAIRS, ts // KEY_BLOCK, LANES, KEY_BLOCK), lambda b, i: (b, 0, i, 0, 0)),
            pl.BlockSpec((1, ts, width), lambda b, i: (b, i, 0)),
        ],
        out_shape=[
            jax.ShapeDtypeStruct((B, S, width), BF16),
            jax.ShapeDtypeStruct((B, S, width), BF16),
            jax.ShapeDtypeStruct((B, HEAD_PAIRS, nkb, LANES, KEY_BLOCK), BF16),
            jax.ShapeDtypeStruct((B, S, width), BF16),
        ],
        scratch_shapes=[pltpu.VMEM((ts + 8, width), F32)],
        compiler_params=pltpu.CompilerParams(
            dimension_semantics=("arbitrary", "arbitrary"), vmem_limit_bytes=VMEM_LIMIT),
        name="proj_conv_qknorm",
    )(x, g1, w_in, conv_w, gq, gk, gc)


def _attn_kernel(q_ref, kt_ref, v_ref, tm_ref, ga_ref, o_ref, acc_ref, carry_ref, left_ref, *, tq):
    qi = pl.program_id(2)
    lane_lo = lax.broadcasted_iota(jnp.int32, (tq, LANES), 1) < HEAD_DIM
    q = q_ref[0]
    zero = jnp.zeros_like(q)
    qh = (jnp.where(lane_lo, q, zero), jnp.where(lane_lo, zero, q))
    acc_ref[...] = jnp.zeros_like(acc_ref)
    carry_ref[...] = jnp.zeros_like(carry_ref)

    n_sub = tq // KEY_BLOCK

    def key_block(j, s, diagonal):
        rows = slice(s * KEY_BLOCK, (s + 1) * KEY_BLOCK)
        kt = kt_ref[0, 0, j]
        start = pl.multiple_of(j * KEY_BLOCK, KEY_BLOCK)
        vb = v_ref[0, pl.ds(start, KEY_BLOCK), :]
        for h in range(2):
            z = jnp.dot(qh[h][rows, :], kt, preferred_element_type=F32)
            sp = jnp.maximum(jnp.log(1.0 + jnp.exp(jnp.minimum(z, 80.0))), z)
            if diagonal:
                row = lax.broadcasted_iota(jnp.int32, (KEY_BLOCK, KEY_BLOCK), 0)
                col = lax.broadcasted_iota(jnp.int32, (KEY_BLOCK, KEY_BLOCK), 1)
                sp = jnp.where(col < row, sp, 0.0)
                z = jnp.where(col < row, z, MASKED)
            r = jnp.dot(sp.astype(BF16), tm_ref[...], preferred_element_type=F32)
            carry = carry_ref[h, rows, :]
            a = jnp.exp(z - (r + jnp.concatenate([carry] * (KEY_BLOCK // LANES), axis=1)))
            carry = carry + jnp.broadcast_to(r[:, 0:1], carry.shape)
            carry_ref[h, rows, :] = carry
            left_ref[h * n_sub + s:h * n_sub + s + 1, :] = jnp.min(carry, axis=0, keepdims=True)
            acc_ref[h, rows, :] += jnp.dot(a.astype(BF16), vb, preferred_element_type=F32)

    first = qi * n_sub

    @pl.when(qi == 0)
    def _():
        key_block(0, 0, True)
        for s in range(1, n_sub):
            key_block(s, s, True)
            key_block(s - 1, s, False)

    @pl.when(qi > 0)
    def _():
        for s in range(n_sub):
            key_block(first + s, s, True)
            key_block(first + s - 1, s, False)

    def stick_left(t):
        return jnp.logical_and(first + n_sub - 3 - t >= 0, jnp.min(left_ref[...]) < STICK_GONE)

    def step(t):
        for s in range(n_sub):
            j = first + s - 2 - t

            @pl.when(j >= 0)
            def _():
                key_block(j, s, False)
        return t + 1

    lax.while_loop(stick_left, step, 0)

    o = jnp.where(lane_lo, acc_ref[0], acc_ref[1])
    o_ref[0] = (_pair_group_rms(o, lane_lo) * ga_ref[...]).astype(BF16)


def _attn_call(q, kt, v, tmat, ga, *, tq):
    B, S, width = q.shape
    assert tq % KEY_BLOCK == 0 and S % tq == 0
    kern = functools.partial(_attn_kernel, tq=tq)
    return pl.pallas_call(
        kern,
        grid=(B, HEAD_PAIRS, S // tq),
        in_specs=[
            pl.BlockSpec((1, tq, LANES), lambda b, p, i: (b, i, p)),
            pl.BlockSpec((1, 1) + kt.shape[2:], lambda b, p, i: (b, p, 0, 0, 0)),
            pl.BlockSpec((1, S, LANES), lambda b, p, i: (b, 0, p)),
            pl.BlockSpec(tmat.shape, lambda b, p, i: (0, 0)),
            pl.BlockSpec((1, LANES), lambda b, p, i: (0, p)),
        ],
        out_specs=pl.BlockSpec((1, tq, LANES), lambda b, p, i: (b, i, p)),
        out_shape=jax.ShapeDtypeStruct((B, S, width), BF16),
        scratch_shapes=[pltpu.VMEM((2, tq, LANES), F32), pltpu.VMEM((2, tq, LANES), F32),
                        pltpu.VMEM((2 * (tq // KEY_BLOCK), LANES), F32)],
        compiler_params=pltpu.CompilerParams(
            dimension_semantics=("arbitrary", "arbitrary", "arbitrary"), vmem_limit_bytes=VMEM_LIMIT),
        name="stickbreak_attn",
    )(q, kt, v, tmat, ga)


def _mix_kernel(yc_ref, ya_ref, x_ref, wo_ref, g2_ref, wrh_ref, wrl_ref, br_ref, tri_ref,
                x1_ref, xn_ref, meta_ref, cnt_ref, *, tm, width):
    rc = tri_ref.shape[0]
    lane = lax.broadcasted_iota(jnp.int32, (rc, LANES), 1)
    neg = jnp.float32(-jnp.inf)

    def first_argmax(vals):
        m = jnp.max(vals, axis=-1, keepdims=True)
        idx = jnp.min(jnp.where(vals == m, lane, LANES), axis=-1, keepdims=True)
        return m, idx

    @pl.when(pl.program_id(0) == 0)
    def _():
        cnt_ref[...] = jnp.zeros_like(cnt_ref)

    seen = cnt_ref[...]
    for c in range(tm // rc):
        rows = slice(c * rc, (c + 1) * rc)
        mix = jnp.dot(yc_ref[rows, :], wo_ref[0:width, :], preferred_element_type=F32)
        mix = mix + jnp.dot(ya_ref[rows, :], wo_ref[width:2 * width, :], preferred_element_type=F32)
        x1 = x_ref[rows, :] + mix
        x1_ref[rows, :] = x1
        ms = jnp.mean(x1 * x1, axis=-1, keepdims=True)
        xn = x1 * lax.rsqrt(ms + EPS) * g2_ref[...]
        xn_ref[rows] = xn.reshape(rc, ROW_TILE, LANES)
        xh = xn.astype(BF16)
        xl = (xn - xh.astype(F32)).astype(BF16)
        logits = (jnp.dot(xh, wrh_ref[...], preferred_element_type=F32)
                  + jnp.dot(xl, wrh_ref[...], preferred_element_type=F32)
                  + jnp.dot(xh, wrl_ref[...], preferred_element_type=F32)) + br_ref[...]
        el = logits[:, :LANES]
        gl = logits[:, LANES:]
        glm = jnp.where(lane < N_GROUPS, gl, neg)
        gmax, gidx = first_argmax(glm)
        p_g = 1.0 / jnp.sum(jnp.exp(glm - gmax), axis=-1, keepdims=True)
        elm = jnp.where((lane // EXPERTS_PER_GROUP) == gidx, el, neg)
        v1, i1 = first_argmax(elm)
        v2, i2 = first_argmax(jnp.where(lane == i1, neg, elm))
        w = jnp.exp(v2 - v1)
        gate1 = p_g / (1.0 + w)
        gate2 = p_g * w / (1.0 + w)

        o1 = lane == i1
        o2 = lane == i2
        osum = jnp.where(jnp.logical_or(o1, o2), 1.0, 0.0)
        base = jnp.dot(tri_ref[...], osum.astype(BF16), preferred_element_type=F32) + seen
        rank1 = jnp.sum(jnp.where(o1, base, 0.0), axis=-1, keepdims=True)
        rank2 = jnp.sum(jnp.where(o2, base, 0.0), axis=-1, keepdims=True)
        seen = seen + jnp.sum(osum, axis=0, keepdims=True)

        cols = (i1.astype(F32), i2.astype(F32), rank1, rank2, gate1, gate2)
        meta = jnp.zeros((rc, LANES), F32)
        for n, cval in enumerate(cols):
            meta = jnp.where(lane == n, cval, meta)
        meta_ref[rows, :] = meta
    cnt_ref[...] = seen


def _mix_call(yc, ya, x2d, w_out, g2, wr_hi, wr_lo, br, tri, *, tm):
    N, D = x2d.shape
    width = yc.shape[1]
    kern = functools.partial(_mix_kernel, tm=tm, width=width)
    row = lambda i: (i, 0)
    const = lambda i: (0, 0)
    return pl.pallas_call(
        kern,
        grid=(N // tm,),
        in_specs=[
            pl.BlockSpec((tm, width), row),
            pl.BlockSpec((tm, width), row),
            pl.BlockSpec((tm, D), row),
            pl.BlockSpec(w_out.shape, const),
            pl.BlockSpec((1, D), const),
            pl.BlockSpec(wr_hi.shape, const),
            pl.BlockSpec(wr_lo.shape, const),
            pl.BlockSpec(br.shape, const),
            pl.BlockSpec(tri.shape, const),
        ],
        out_specs=[
            pl.BlockSpec((tm, D), row),
            pl.BlockSpec((tm, ROW_TILE, LANES), lambda i: (i, 0, 0)),
            pl.BlockSpec((tm, LANES), row),
            pl.BlockSpec((1, LANES), const),
        ],
        out_shape=[
            jax.ShapeDtypeStruct((N, D), F32),
            jax.ShapeDtypeStruct((N, ROW_TILE, LANES), F32),
            jax.ShapeDtypeStruct((N, LANES), F32),
            jax.ShapeDtypeStruct((1, LANES), F32),
        ],
        compiler_params=pltpu.CompilerParams(
            dimension_semantics=("arbitrary",), vmem_limit_bytes=VMEM_LIMIT),
        name="outproj_router",
    )(yc, ya, x2d, w_out, g2, wr_hi, wr_lo, br, tri)


def _dispatch_kernel(pstart_ref, cnt_ref, pend_ref, dest_ref, xn_ref, xin_ref, zblk, sem, zsem, *, td):
    del pstart_ref
    n_blocks = xin_ref.shape[0] // DISPATCH_BLOCK

    @pl.when(pl.program_id(0) == 0)
    def _():
        zblk[...] = jnp.zeros_like(zblk)

        def zero_block(b):
            start = pl.multiple_of(b * DISPATCH_BLOCK, DISPATCH_BLOCK)
            return pltpu.make_async_copy(zblk, xin_ref.at[pl.ds(start, DISPATCH_BLOCK)], zsem)

        def padded_blocks(fn):
            def per_expert(e, _):
                @pl.when(cnt_ref[e] % DISPATCH_BLOCK != 0)
                def _():
                    fn(zero_block(pend_ref[e] // DISPATCH_BLOCK - 1))
                return 0

            def unused(b, _):
                fn(zero_block(b))
                return 0

            lax.fori_loop(0, N_EXPERTS, per_expert, 0)
            lax.fori_loop(pend_ref[N_EXPERTS - 1] // DISPATCH_BLOCK, n_blocks, unused, 0)

        padded_blocks(lambda copy: copy.start())
        padded_blocks(lambda copy: copy.wait())

    def row_copy(r, k):
        return pltpu.make_async_copy(xn_ref.at[r], xin_ref.at[dest_ref[TOP_K * r + k]], sem)

    def start_rows(r, _):
        for k in range(TOP_K):
            row_copy(r, k).start(priority=k % N_DMA_QUEUES)
        return 0

    def wait_rows(r, _):
        for k in range(TOP_K):
            row_copy(r, k).wait()
        return 0

    lax.fori_loop(0, td, start_rows, 0, unroll=DMA_UNROLL)
    lax.fori_loop(0, td, wait_rows, 0, unroll=DMA_UNROLL)


def _dispatch_call(pstart, cnt, pend, dest_flat, xn, n_rows, *, td):
    N = xn.shape[0]
    kern = functools.partial(_dispatch_kernel, td=td)
    smem_block = pl.BlockSpec((TOP_K * td,), lambda i, *_: (i,), memory_space=pltpu.SMEM)
    grid_spec = pltpu.PrefetchScalarGridSpec(
        num_scalar_prefetch=3,
        grid=(N // td,),
        in_specs=[smem_block, pl.BlockSpec((td, ROW_TILE, LANES), lambda i, *_: (i, 0, 0))],
        out_specs=pl.BlockSpec(memory_space=pl.ANY),
        scratch_shapes=[pltpu.VMEM((DISPATCH_BLOCK, ROW_TILE, LANES), xn.dtype), pltpu.SemaphoreType.DMA(()),
                        pltpu.SemaphoreType.DMA(())],
    )
    return pl.pallas_call(
        kern,
        grid_spec=grid_spec,
        out_shape=jax.ShapeDtypeStruct((n_rows,) + xn.shape[1:], xn.dtype),
        compiler_params=pltpu.CompilerParams(dimension_semantics=("arbitrary",)),
        name="moe_dispatch",
    )(pstart, cnt, pend, dest_flat, xn)


def _ffn_kernel(blk_e_ref, n_used_ref, x_ref, w1_ref, w3_ref, w2_ref, y_ref, w1b, w3b, w2b):
    i = pl.program_id(0)
    prev = blk_e_ref[jnp.maximum(i - 1, 0)]
    changed = jnp.logical_or(i == 0, blk_e_ref[i] != prev)

    @pl.when(changed)
    def _():
        w1b[...] = w1_ref[...].astype(BF16)
        w3b[...] = w3_ref[...].astype(BF16)
        w2b[...] = w2_ref[...].astype(BF16)

    @pl.when(i < n_used_ref[0])
    def _():
        x = x_ref[...].reshape(DISPATCH_BLOCK, ROW_TILE * LANES).astype(BF16)
        a = jnp.dot(x, w1b[...], preferred_element_type=F32)
        b = jnp.dot(x, w3b[...], preferred_element_type=F32)
        h = (a * jax.nn.sigmoid(a) * b).astype(BF16)
        y = jnp.dot(h, w2b[...], preferred_element_type=F32)
        y_ref[...] = y.reshape(DISPATCH_BLOCK, ROW_TILE, LANES)

    @pl.when(i >= n_used_ref[0])
    def _():
        y_ref[...] = jnp.zeros_like(y_ref)


def _ffn_call(blk_e, n_used, xin, w1, w3, w2):
    P = xin.shape[0]
    E, D, DE = w1.shape
    n_blocks = P // DISPATCH_BLOCK
    tok_block = (DISPATCH_BLOCK, ROW_TILE, LANES)
    grid_spec = pltpu.PrefetchScalarGridSpec(
        num_scalar_prefetch=2,
        grid=(n_blocks,),
        in_specs=[
            pl.BlockSpec(tok_block, lambda i, be, nu: (jnp.minimum(i, nu[0] - 1), 0, 0)),
            pl.BlockSpec((None, D, DE), lambda i, be, nu: (be[i], 0, 0)),
            pl.BlockSpec((None, D, DE), lambda i, be, nu: (be[i], 0, 0)),
            pl.BlockSpec((None, DE, D), lambda i, be, nu: (be[i], 0, 0)),
        ],
        out_specs=pl.BlockSpec(tok_block, lambda i, be, nu: (i, 0, 0)),
        scratch_shapes=[pltpu.VMEM((D, DE), BF16), pltpu.VMEM((D, DE), BF16), pltpu.VMEM((DE, D), BF16)],
    )
    return pl.pallas_call(
        _ffn_kernel,
        grid_spec=grid_spec,
        out_shape=jax.ShapeDtypeStruct(xin.shape, F32),
        compiler_params=pltpu.CompilerParams(
            dimension_semantics=("arbitrary",), vmem_limit_bytes=VMEM_LIMIT),
        name="moe_expert_ffn",
    )(blk_e, n_used, xin, w1, w3, w2)


def _combine_kernel(dest_ref, dest_next_ref, y_ref, x1_ref, meta_ref, o_ref, ybuf, sems, *, tc):
    i = pl.program_id(0)
    slot = i % 2

    def gather(d_ref, s):
        def start(r, _):
            for k in range(TOP_K):
                pltpu.make_async_copy(y_ref.at[d_ref[TOP_K * r + k]], ybuf.at[s, k, r], sems.at[s]).start(
                    priority=k % N_DMA_QUEUES)
            return 0
        lax.fori_loop(0, tc, start, 0, unroll=DMA_UNROLL)

    @pl.when(i == 0)
    def _():
        gather(dest_ref, 0)

    @pl.when(i + 1 < pl.num_programs(0))
    def _():
        gather(dest_next_ref, 1 - slot)

    def wait(r, _):
        for k in range(TOP_K):
            pltpu.make_async_copy(y_ref.at[0], ybuf.at[slot, k, r], sems.at[slot]).wait()
        return 0

    lax.fori_loop(0, tc, wait, 0, unroll=DMA_UNROLL)
    meta = meta_ref[...]
    y1 = ybuf[slot, 0].reshape(tc, ROW_TILE * LANES)
    y2 = ybuf[slot, 1].reshape(tc, ROW_TILE * LANES)
    o_ref[...] = x1_ref[...] + (y1 * meta[:, 4:5] + y2 * meta[:, 5:6])


def _combine_call(dest_flat, y, x1, meta, *, tc):
    N, D = x1.shape
    n_tiles = N // tc
    kern = functools.partial(_combine_kernel, tc=tc)
    return pl.pallas_call(
        kern,
        grid=(n_tiles,),
        in_specs=[
            pl.BlockSpec((TOP_K * tc,), lambda i: (i,), memory_space=pltpu.SMEM),
            pl.BlockSpec((TOP_K * tc,), lambda i: (jnp.minimum(i + 1, n_tiles - 1),), memory_space=pltpu.SMEM),
            pl.BlockSpec(memory_space=pl.ANY),
            pl.BlockSpec((tc, D), lambda i: (i, 0)),
            pl.BlockSpec((tc, LANES), lambda i: (i, 0)),
        ],
        out_specs=pl.BlockSpec((tc, D), lambda i: (i, 0)),
        out_shape=jax.ShapeDtypeStruct((N, D), F32),
        scratch_shapes=[pltpu.VMEM((2, TOP_K, tc, ROW_TILE, LANES), F32), pltpu.SemaphoreType.DMA((2,))],
        compiler_params=pltpu.CompilerParams(
            dimension_semantics=("arbitrary",), vmem_limit_bytes=VMEM_LIMIT),
        name="moe_combine",
    )(dest_flat, dest_flat, y, x1, meta)


def _suffix_matrix():
    r = np.arange(KEY_BLOCK)[:, None]
    c = np.arange(KEY_BLOCK)[None, :]
    return jnp.asarray(r >= c, BF16)


def _layer(x, norm_mix_w, w_in, conv_w, q_norm_w, k_norm_w, gn_conv_w, gn_attn_w, w_out,
           norm_ffn_w, w_group, b_group, w_expert, b_expert, w1, w3, w2):
    B, S, D = x.shape
    N = B * S
    ts = min(512, S)
    tq = min(4 * KEY_BLOCK, S)
    tm = min(512, N)
    scale = 1.0 / float(np.sqrt(HEAD_DIM))

    gq = (jnp.tile(q_norm_w, N_HEADS) * scale)[None, :].astype(F32)
    gk = jnp.tile(k_norm_w, N_HEADS)[None, :].astype(F32)
    gc = gn_conv_w.reshape(1, -1).astype(F32)
    ga = gn_attn_w.reshape(1, -1).astype(F32)
    yc, q, kt, v = _proj_call(x, norm_mix_w[None, :], w_in.astype(BF16), conv_w, gq, gk, gc, ts=ts)
    ya = _attn_call(q, kt, v, _suffix_matrix(), ga, tq=tq)

    pad_lanes = lambda a: jnp.pad(a.astype(F32), ((0, 0), (0, LANES - a.shape[1])))
    wr = jnp.concatenate([pad_lanes(w_expert), pad_lanes(w_group)], axis=1)
    br = jnp.concatenate([pad_lanes(b_expert[None, :]), pad_lanes(b_group[None, :])], axis=1)
    wr_hi = wr.astype(BF16)
    wr_lo = (wr - wr_hi.astype(F32)).astype(BF16)
    rc = min(ROUTE_CHUNK, tm)
    tri = jnp.asarray(np.tril(np.ones((rc, rc), np.float32), -1), BF16)
    x1, xn2, meta, counts = _mix_call(
        yc.reshape(N, -1), ya.reshape(N, -1), x.reshape(N, D), w_out.astype(BF16),
        norm_ffn_w[None, :], wr_hi, wr_lo, br, tri, tm=tm)

    eid = meta[:, 0:TOP_K].astype(jnp.int32).reshape(-1)
    rank = meta[:, TOP_K:2 * TOP_K].astype(jnp.int32).reshape(-1)
    cnt = counts[0, :N_EXPERTS].astype(jnp.int32)
    pcnt = (cnt + DISPATCH_BLOCK - 1) // DISPATCH_BLOCK * DISPATCH_BLOCK
    pend = jnp.cumsum(pcnt)
    pstart = pend - pcnt
    n_blocks = -(-N * TOP_K // DISPATCH_BLOCK) + N_EXPERTS
    blk_start = jnp.arange(n_blocks, dtype=jnp.int32) * DISPATCH_BLOCK
    blk_e = jnp.sum((blk_start[:, None] >= pend[None, :]).astype(jnp.int32), axis=1)
    blk_e = jnp.minimum(blk_e, N_EXPERTS - 1)
    n_used = pend[-1:] // DISPATCH_BLOCK
    onehot = eid[:, None] == jnp.arange(N_EXPERTS, dtype=jnp.int32)[None, :]
    dest = rank + jnp.sum(jnp.where(onehot, pstart[None, :], 0), axis=1)

    xin = _dispatch_call(pstart, cnt, pend, dest, xn2, n_blocks * DISPATCH_BLOCK, td=min(512, N))
    y = _ffn_call(blk_e, n_used, xin, w1, w3, w2)
    out = _combine_call(dest, y, x1, meta, tc=min(256, N))
    return out.reshape(B, S, D)


def kernel(x, norm_mix_w, w_in, conv_w, q_norm_w, k_norm_w, group_norm_conv_w, group_norm_attn_w, w_out,
           norm_ffn_w, w_group, b_group, w_expert, b_expert, w1, w3, w2):
    depth = norm_mix_w.shape[0]
    for l in range(depth):
        x = _layer(x, norm_mix_w[l], w_in[l], conv_w[l], q_norm_w[l], k_norm_w[l], group_norm_conv_w[l],
                   group_norm_attn_w[l], w_out[l], norm_ffn_w[l], w_group[l], b_group[l], w_expert[l],
                   b_expert[l], w1[l], w3[l], w2[l])
    return x
```

```python
import functools

import numpy as np
import jax
import jax.numpy as jnp
from jax import lax
from jax.experimental import pallas as pl
from jax.experimental.pallas import tpu as pltpu

F32 = jnp.float32
BF16 = jnp.bfloat16

EPS = 1e-6
MASKED = -1e30
STICK_GONE = 110.0
LANES = 128
HEAD_DIM = 64
N_HEADS = 8
N_CONV_GROUPS = 8
CONV_K = 3
N_GROUPS = 4
EXPERTS_PER_GROUP = 8
N_EXPERTS = N_GROUPS * EXPERTS_PER_GROUP
TOP_K = 2
DISPATCH_BLOCK = 512
HEAD_PAIRS = N_HEADS * HEAD_DIM // LANES

ROW_TILE = 8
KEY_BLOCK = 256
DMA_UNROLL = 4
N_DMA_QUEUES = 2
ROUTE_CHUNK = 512
VMEM_LIMIT = 56 * 1024 * 1024


def _pair_group_rms(x, lane_lo):
    x2 = x * x
    s_lo = jnp.sum(jnp.where(lane_lo, x2, 0.0), axis=-1, keepdims=True)
    s_hi = jnp.sum(jnp.where(lane_lo, 0.0, x2), axis=-1, keepdims=True)
    ms = jnp.where(lane_lo, s_lo, s_hi) * (1.0 / HEAD_DIM)
    return x * lax.rsqrt(ms + EPS)


def _proj_kernel(x_ref, g1_ref, win_ref, cw_ref, gq_ref, gk_ref, gc_ref,
                 yc_ref, q_ref, kt_ref, v_ref, ubuf, *, ts, width):
    i = pl.program_id(1)
    x = x_ref[0]
    ms = jnp.mean(x * x, axis=-1, keepdims=True)
    xn = (x * lax.rsqrt(ms + EPS) * g1_ref[...]).astype(BF16)

    def proj(c):
        return jnp.dot(xn, win_ref[:, c * width:(c + 1) * width], preferred_element_type=F32)

    lane_lo = lax.broadcasted_iota(jnp.int32, (ts, LANES), 1) < HEAD_DIM

    u = proj(1) * proj(2)

    @pl.when(i == 0)
    def _():
        ubuf[0:8, :] = jnp.zeros((8, width), F32)

    ubuf[8:8 + ts, :] = u
    u1 = ubuf[7:7 + ts, :]
    u2 = ubuf[6:6 + ts, :]
    ubuf[0:8, :] = u[ts - 8:ts, :]
    y = cw_ref[0:1, :] * u2 + cw_ref[1:2, :] * u1 + cw_ref[2:3, :] * u
    yc = proj(0) * y
    for c in range(width // LANES):
        sl = slice(c * LANES, (c + 1) * LANES)
        yc_ref[0, :, sl] = (_pair_group_rms(yc[:, sl], lane_lo) * gc_ref[:, sl]).astype(BF16)

    q = proj(3)
    for c in range(width // LANES):
        sl = slice(c * LANES, (c + 1) * LANES)
        q_ref[0, :, sl] = (_pair_group_rms(q[:, sl], lane_lo) * gq_ref[:, sl]).astype(BF16)

    k = proj(4)
    for c in range(width // LANES):
        sl = slice(c * LANES, (c + 1) * LANES)
        kn = _pair_group_rms(k[:, sl], lane_lo) * gk_ref[:, sl]
        knt = kn.T.astype(BF16)
        for j in range(ts // KEY_BLOCK):
            kt_ref[0, c, j] = knt[:, j * KEY_BLOCK:(j + 1) * KEY_BLOCK]

    v_ref[0] = proj(5).astype(BF16)


def _proj_call(x, g1, w_in, conv_w, gq, gk, gc, *, ts):
    B, S, D = x.shape
    width = w_in.shape[1] // 6
    nkb = S // KEY_BLOCK
    kern = functools.partial(_proj_kernel, ts=ts, width=width)
    const = lambda b, i: (0, 0)
    return pl.pallas_call(
        kern,
        grid=(B, S // ts),
        in_specs=[
            pl.BlockSpec((1, ts, D), lambda b, i: (b, i, 0)),
            pl.BlockSpec((1, D), const),
            pl.BlockSpec(w_in.shape, const),
            pl.BlockSpec(conv_w.shape, const),
            pl.BlockSpec((1, width), const),
            pl.BlockSpec((1, width), const),
            pl.BlockSpec((1, width), const),
        ],
        out_specs=[
            pl.BlockSpec((1, ts, width), lambda b, i: (b, i, 0)),
            pl.BlockSpec((1, ts, width), lambda b, i: (b, i, 0)),
            pl.BlockSpec((1, HEAD_PAIRS, ts // KEY_BLOCK, LANES, KEY_BLOCK), lambda b, i: (b, 0, i, 0, 0)),
            pl.BlockSpec((1, ts, width), lambda b, i: (b, i, 0)),
        ],
        out_shape=[
            jax.ShapeDtypeStruct((B, S, width), BF16),
            jax.ShapeDtypeStruct((B, S, width), BF16),
            jax.ShapeDtypeStruct((B, HEAD_PAIRS, nkb, LANES, KEY_BLOCK), BF16),
            jax.ShapeDtypeStruct((B, S, width), BF16),
        ],
        scratch_shapes=[pltpu.VMEM((ts + 8, width), F32)],
        compiler_params=pltpu.CompilerParams(
            dimension_semantics=("arbitrary", "arbitrary"), vmem_limit_bytes=VMEM_LIMIT),
        name="proj_conv_qknorm",
    )(x, g1, w_in, conv_w, gq, gk, gc)


def _attn_kernel(q_ref, kt_ref, v_ref, tm_ref, ga_ref, o_ref, acc_ref, carry_ref, left_ref, *, tq):
    qi = pl.program_id(2)
    lane_lo = lax.broadcasted_iota(jnp.int32, (tq, LANES), 1) < HEAD_DIM
    q = q_ref[0]
    zero = jnp.zeros_like(q)
    qh = (jnp.where(lane_lo, q, zero), jnp.where(lane_lo, zero, q))
    acc_ref[...] = jnp.zeros_like(acc_ref)
    carry_ref[...] = jnp.zeros_like(carry_ref)

    n_sub = tq // KEY_BLOCK

    def key_block(j, s, diagonal):
        rows = slice(s * KEY_BLOCK, (s + 1) * KEY_BLOCK)
        kt = kt_ref[0, 0, j]
        start = pl.multiple_of(j * KEY_BLOCK, KEY_BLOCK)
        vb = v_ref[0, pl.ds(start, KEY_BLOCK), :]
        for h in range(2):
            z = jnp.dot(qh[h][rows, :], kt, preferred_element_type=F32)
            sp = jnp.maximum(jnp.log(1.0 + jnp.exp(jnp.minimum(z, 80.0))), z)
            if diagonal:
                row = lax.broadcasted_iota(jnp.int32, (KEY_BLOCK, KEY_BLOCK), 0)
                col = lax.broadcasted_iota(jnp.int32, (KEY_BLOCK, KEY_BLOCK), 1)
                sp = jnp.where(col < row, sp, 0.0)
                z = jnp.where(col < row, z, MASKED)
            r = jnp.dot(sp.astype(BF16), tm_ref[...], preferred_element_type=F32)
            carry = carry_ref[h, rows, :]
            a = jnp.exp(z - (r + jnp.concatenate([carry] * (KEY_BLOCK // LANES), axis=1)))
            carry = carry + jnp.broadcast_to(r[:, 0:1], carry.shape)
            carry_ref[h, rows, :] = carry
            left_ref[h * n_sub + s:h * n_sub + s + 1, :] = jnp.min(carry, axis=0, keepdims=True)
            acc_ref[h, rows, :] += jnp.dot(a.astype(BF16), vb, preferred_element_type=F32)

    first = qi * n_sub

    @pl.when(qi == 0)
    def _():
        key_block(0, 0, True)
        for s in range(1, n_sub):
            key_block(s, s, True)
            key_block(s - 1, s, False)

    @pl.when(qi > 0)
    def _():
        for s in range(n_sub):
            key_block(first + s, s, True)
            key_block(first + s - 1, s, False)

    def stick_left(t):
        return jnp.logical_and(first + n_sub - 3 - t >= 0, jnp.min(left_ref[...]) < STICK_GONE)

    def step(t):
        for s in range(n_sub):
            j = first + s - 2 - t

            @pl.when(j >= 0)
            def _():
                key_block(j, s, False)
        return t + 1

    lax.while_loop(stick_left, step, 0)

    o = jnp.where(lane_lo, acc_ref[0], acc_ref[1])
    o_ref[0] = (_pair_group_rms(o, lane_lo) * ga_ref[...]).astype(BF16)


def _attn_call(q, kt, v, tmat, ga, *, tq):
    B, S, width = q.shape
    assert tq % KEY_BLOCK == 0 and S % tq == 0
    kern = functools.partial(_attn_kernel, tq=tq)
    return pl.pallas_call(
        kern,
        grid=(B, HEAD_PAIRS, S // tq),
        in_specs=[
            pl.BlockSpec((1, tq, LANES), lambda b, p, i: (b, i, p)),
            pl.BlockSpec((1, 1) + kt.shape[2:], lambda b, p, i: (b, p, 0, 0, 0)),
            pl.BlockSpec((1, S, LANES), lambda b, p, i: (b, 0, p)),
            pl.BlockSpec(tmat.shape, lambda b, p, i: (0, 0)),
            pl.BlockSpec((1, LANES), lambda b, p, i: (0, p)),
        ],
        out_specs=pl.BlockSpec((1, tq, LANES), lambda b, p, i: (b, i, p)),
        out_shape=jax.ShapeDtypeStruct((B, S, width), BF16),
        scratch_shapes=[pltpu.VMEM((2, tq, LANES), F32), pltpu.VMEM((2, tq, LANES), F32),
                        pltpu.VMEM((2 * (tq // KEY_BLOCK), LANES), F32)],
        compiler_params=pltpu.CompilerParams(
            dimension_semantics=("arbitrary", "arbitrary", "arbitrary"), vmem_limit_bytes=VMEM_LIMIT),
        name="stickbreak_attn",
    )(q, kt, v, tmat, ga)


def _mix_kernel(yc_ref, ya_ref, x_ref, wo_ref, g2_ref, wrh_ref, wrl_ref, br_ref, tri_ref,
                x1_ref, xn_ref, meta_ref, cnt_ref, xn_prev, *, tm, width):
    i = pl.program_id(0)
    lane = lax.broadcasted_iota(jnp.int32, (tm, LANES), 1)
    neg = jnp.float32(-jnp.inf)

    def first_argmax(vals):
        m = jnp.max(vals, axis=-1, keepdims=True)
        idx = jnp.min(jnp.where(vals == m, lane, LANES), axis=-1, keepdims=True)
        return m, idx

    @pl.when(i == 0)
    def _():
        cnt_ref[...] = jnp.zeros_like(cnt_ref)
        xn_prev[...] = jnp.zeros_like(xn_prev)

    xp = xn_prev[...]
    xh = xp.astype(BF16)
    xl = (xp - xh.astype(F32)).astype(BF16)
    logits = (jnp.dot(xh, wrh_ref[...], preferred_element_type=F32)
              + jnp.dot(xl, wrh_ref[...], preferred_element_type=F32)
              + jnp.dot(xh, wrl_ref[...], preferred_element_type=F32)) + br_ref[...]
    el = logits[:, :LANES]
    gl = logits[:, LANES:]
    glm = jnp.where(lane < N_GROUPS, gl, neg)
    gmax, gidx = first_argmax(glm)
    p_g = 1.0 / jnp.sum(jnp.exp(glm - gmax), axis=-1, keepdims=True)
    elm = jnp.where((lane // EXPERTS_PER_GROUP) == gidx, el, neg)
    v1, i1 = first_argmax(elm)
    v2, i2 = first_argmax(jnp.where(lane == i1, neg, elm))
    w = jnp.exp(v2 - v1)
    gate1 = p_g / (1.0 + w)
    gate2 = p_g * w / (1.0 + w)

    o1 = lane == i1
    o2 = lane == i2
    osum = jnp.where(jnp.logical_or(o1, o2), 1.0, 0.0)
    seen = cnt_ref[...]
    base = jnp.dot(tri_ref[...], osum.astype(BF16), preferred_element_type=F32) + seen
    rank1 = jnp.sum(jnp.where(o1, base, 0.0), axis=-1, keepdims=True)
    rank2 = jnp.sum(jnp.where(o2, base, 0.0), axis=-1, keepdims=True)
    counted = jnp.where(i > 0, 1.0, 0.0)
    cnt_ref[...] = seen + counted * jnp.sum(osum, axis=0, keepdims=True)

    cols = (i1.astype(F32), i2.astype(F32), rank1, rank2, gate1, gate2)
    meta = jnp.zeros((tm, LANES), F32)
    for n, cval in enumerate(cols):
        meta = jnp.where(lane == n, cval, meta)
    meta_ref[...] = meta

    mix = jnp.dot(yc_ref[...], wo_ref[0:width, :], preferred_element_type=F32)
    mix = mix + jnp.dot(ya_ref[...], wo_ref[width:2 * width, :], preferred_element_type=F32)
    x1 = x_ref[...] + mix
    x1_ref[...] = x1
    ms = jnp.mean(x1 * x1, axis=-1, keepdims=True)
    xn = x1 * lax.rsqrt(ms + EPS) * g2_ref[...]
    xn_ref[...] = xn.reshape(tm, ROW_TILE, LANES)
    xn_prev[...] = xn


def _mix_call(yc, ya, x2d, w_out, g2, wr_hi, wr_lo, br, tri, *, tm):
    N, D = x2d.shape
    width = yc.shape[1]
    n_tiles = N // tm
    kern = functools.partial(_mix_kernel, tm=tm, width=width)
    cur = lambda i: (jnp.minimum(i, n_tiles - 1), 0)
    prev = lambda i: (jnp.maximum(i - 1, 0), 0)
    const = lambda i: (0, 0)
    return pl.pallas_call(
        kern,
        grid=(n_tiles + 1,),
        in_specs=[
            pl.BlockSpec((tm, width), cur),
            pl.BlockSpec((tm, width), cur),
            pl.BlockSpec((tm, D), cur),
            pl.BlockSpec(w_out.shape, const),
            pl.BlockSpec((1, D), const),
            pl.BlockSpec(wr_hi.shape, const),
            pl.BlockSpec(wr_lo.shape, const),
            pl.BlockSpec(br.shape, const),
            pl.BlockSpec(tri.shape, const),
        ],
        out_specs=[
            pl.BlockSpec((tm, D), cur),
            pl.BlockSpec((tm, ROW_TILE, LANES), lambda i: (jnp.minimum(i, n_tiles - 1), 0, 0)),
            pl.BlockSpec((tm, LANES), prev),
            pl.BlockSpec((1, LANES), const),
        ],
        out_shape=[
            jax.ShapeDtypeStruct((N, D), F32),
            jax.ShapeDtypeStruct((N, ROW_TILE, LANES), F32),
            jax.ShapeDtypeStruct((N, LANES), F32),
            jax.ShapeDtypeStruct((1, LANES), F32),
        ],
        scratch_shapes=[pltpu.VMEM((tm, D), F32)],
        compiler_params=pltpu.CompilerParams(
            dimension_semantics=("arbitrary",), vmem_limit_bytes=VMEM_LIMIT),
        name="outproj_router",
    )(yc, ya, x2d, w_out, g2, wr_hi, wr_lo, br, tri)


def _dispatch_kernel(pstart_ref, cnt_ref, pend_ref, dest_ref, xn_ref, xin_ref, zblk, sem, zsem, *, td):
    del pstart_ref
    n_blocks = xin_ref.shape[0] // DISPATCH_BLOCK

    @pl.when(pl.program_id(0) == 0)
    def _():
        zblk[...] = jnp.zeros_like(zblk)

        def zero_block(b):
            start = pl.multiple_of(b * DISPATCH_BLOCK, DISPATCH_BLOCK)
            return pltpu.make_async_copy(zblk, xin_ref.at[pl.ds(start, DISPATCH_BLOCK)], zsem)

        def padded_blocks(fn):
            def per_expert(e, _):
                @pl.when(cnt_ref[e] % DISPATCH_BLOCK != 0)
                def _():
                    fn(zero_block(pend_ref[e] // DISPATCH_BLOCK - 1))
                return 0

            def unused(b, _):
                fn(zero_block(b))
                return 0

            lax.fori_loop(0, N_EXPERTS, per_expert, 0)
            lax.fori_loop(pend_ref[N_EXPERTS - 1] // DISPATCH_BLOCK, n_blocks, unused, 0)

        padded_blocks(lambda copy: copy.start())
        padded_blocks(lambda copy: copy.wait())

    def row_copy(r, k):
        return pltpu.make_async_copy(xn_ref.at[r], xin_ref.at[dest_ref[TOP_K * r + k]], sem)

    def start_rows(r, _):
        for k in range(TOP_K):
            row_copy(r, k).start(priority=k % N_DMA_QUEUES)
        return 0

    def wait_rows(r, _):
        for k in range(TOP_K):
            row_copy(r, k).wait()
        return 0

    lax.fori_loop(0, td, start_rows, 0, unroll=DMA_UNROLL)
    lax.fori_loop(0, td, wait_rows, 0, unroll=DMA_UNROLL)


def _dispatch_call(pstart, cnt, pend, dest_flat, xn, n_rows, *, td):
    N = xn.shape[0]
    kern = functools.partial(_dispatch_kernel, td=td)
    smem_block = pl.BlockSpec((TOP_K * td,), lambda i, *_: (i,), memory_space=pltpu.SMEM)
    grid_spec = pltpu.PrefetchScalarGridSpec(
        num_scalar_prefetch=3,
        grid=(N // td,),
        in_specs=[smem_block, pl.BlockSpec((td, ROW_TILE, LANES), lambda i, *_: (i, 0, 0))],
        out_specs=pl.BlockSpec(memory_space=pl.ANY),
        scratch_shapes=[pltpu.VMEM((DISPATCH_BLOCK, ROW_TILE, LANES), xn.dtype), pltpu.SemaphoreType.DMA(()),
                        pltpu.SemaphoreType.DMA(())],
    )
    return pl.pallas_call(
        kern,
        grid_spec=grid_spec,
        out_shape=jax.ShapeDtypeStruct((n_rows,) + xn.shape[1:], xn.dtype),
        compiler_params=pltpu.CompilerParams(dimension_semantics=("arbitrary",)),
        name="moe_dispatch",
    )(pstart, cnt, pend, dest_flat, xn)


def _ffn_kernel(blk_e_ref, n_used_ref, x_ref, w1_ref, w3_ref, w2_ref, y_ref, w1b, w3b, w2b):
    i = pl.program_id(0)
    prev = blk_e_ref[jnp.maximum(i - 1, 0)]
    changed = jnp.logical_or(i == 0, blk_e_ref[i] != prev)

    @pl.when(changed)
    def _():
        w1b[...] = w1_ref[...].astype(BF16)
        w3b[...] = w3_ref[...].astype(BF16)
        w2b[...] = w2_ref[...].astype(BF16)

    @pl.when(i < n_used_ref[0])
    def _():
        x = x_ref[...].reshape(DISPATCH_BLOCK, ROW_TILE * LANES).astype(BF16)
        a = jnp.dot(x, w1b[...], preferred_element_type=F32)
        b = jnp.dot(x, w3b[...], preferred_element_type=F32)
        h = (a * jax.nn.sigmoid(a) * b).astype(BF16)
        y = jnp.dot(h, w2b[...], preferred_element_type=F32)
        y_ref[...] = y.reshape(DISPATCH_BLOCK, ROW_TILE, LANES)

    @pl.when(i >= n_used_ref[0])
    def _():
        y_ref[...] = jnp.zeros_like(y_ref)


def _ffn_call(blk_e, n_used, xin, w1, w3, w2):
    P = xin.shape[0]
    E, D, DE = w1.shape
    n_blocks = P // DISPATCH_BLOCK
    tok_block = (DISPATCH_BLOCK, ROW_TILE, LANES)
    grid_spec = pltpu.PrefetchScalarGridSpec(
        num_scalar_prefetch=2,
        grid=(n_blocks,),
        in_specs=[
            pl.BlockSpec(tok_block, lambda i, be, nu: (jnp.minimum(i, nu[0] - 1), 0, 0)),
            pl.BlockSpec((None, D, DE), lambda i, be, nu: (be[i], 0, 0)),
            pl.BlockSpec((None, D, DE), lambda i, be, nu: (be[i], 0, 0)),
            pl.BlockSpec((None, DE, D), lambda i, be, nu: (be[i], 0, 0)),
        ],
        out_specs=pl.BlockSpec(tok_block, lambda i, be, nu: (i, 0, 0)),
        scratch_shapes=[pltpu.VMEM((D, DE), BF16), pltpu.VMEM((D, DE), BF16), pltpu.VMEM((DE, D), BF16)],
    )
    return pl.pallas_call(
        _ffn_kernel,
        grid_spec=grid_spec,
        out_shape=jax.ShapeDtypeStruct(xin.shape, F32),
        compiler_params=pltpu.CompilerParams(
            dimension_semantics=("arbitrary",), vmem_limit_bytes=VMEM_LIMIT),
        name="moe_expert_ffn",
    )(blk_e, n_used, xin, w1, w3, w2)


def _combine_kernel(dest_ref, dest_next_ref, y_ref, x1_ref, meta_ref, o_ref, ybuf, sems, *, tc):
    i = pl.program_id(0)
    slot = i % 2

    def gather(d_ref, s):
        def start(r, _):
            for k in range(TOP_K):
                pltpu.make_async_copy(y_ref.at[d_ref[TOP_K * r + k]], ybuf.at[s, k, r], sems.at[s]).start(
                    priority=k % N_DMA_QUEUES)
            return 0
        lax.fori_loop(0, tc, start, 0, unroll=DMA_UNROLL)

    @pl.when(i == 0)
    def _():
        gather(dest_ref, 0)

    @pl.when(i + 1 < pl.num_programs(0))
    def _():
        gather(dest_next_ref, 1 - slot)

    def wait(r, _):
        for k in range(TOP_K):
            pltpu.make_async_copy(y_ref.at[0], ybuf.at[slot, k, r], sems.at[slot]).wait()
        return 0

    lax.fori_loop(0, tc, wait, 0, unroll=DMA_UNROLL)
    meta = meta_ref[...]
    y1 = ybuf[slot, 0].reshape(tc, ROW_TILE * LANES)
    y2 = ybuf[slot, 1].reshape(tc, ROW_TILE * LANES)
    o_ref[...] = x1_ref[...] + (y1 * meta[:, 4:5] + y2 * meta[:, 5:6])


def _combine_call(dest_flat, y, x1, meta, *, tc):
    N, D = x1.shape
    n_tiles = N // tc
    kern = functools.partial(_combine_kernel, tc=tc)
    return pl.pallas_call(
        kern,
        grid=(n_tiles,),
        in_specs=[
            pl.BlockSpec((TOP_K * tc,), lambda i: (i,), memory_space=pltpu.SMEM),
            pl.BlockSpec((TOP_K * tc,), lambda i: (jnp.minimum(i + 1, n_tiles - 1),), memory_space=pltpu.SMEM),
            pl.BlockSpec(memory_space=pl.ANY),
            pl.BlockSpec((tc, D), lambda i: (i, 0)),
            pl.BlockSpec((tc, LANES), lambda i: (i, 0)),
        ],
        out_specs=pl.BlockSpec((tc, D), lambda i: (i, 0)),
        out_shape=jax.ShapeDtypeStruct((N, D), F32),
        scratch_shapes=[pltpu.VMEM((2, TOP_K, tc, ROW_TILE, LANES), F32), pltpu.SemaphoreType.DMA((2,))],
        compiler_params=pltpu.CompilerParams(
            dimension_semantics=("arbitrary",), vmem_limit_bytes=VMEM_LIMIT),
        name="moe_combine",
    )(dest_flat, dest_flat, y, x1, meta)


def _suffix_matrix():
    r = np.arange(KEY_BLOCK)[:, None]
    c = np.arange(KEY_BLOCK)[None, :]
    return jnp.asarray(r >= c, BF16)


def _layer(x, norm_mix_w, w_in, conv_w, q_norm_w, k_norm_w, gn_conv_w, gn_attn_w, w_out,
           norm_ffn_w, w_group, b_group, w_expert, b_expert, w1, w3, w2):
    B, S, D = x.shape
    N = B * S
    ts = min(512, S)
    tq = min(4 * KEY_BLOCK, S)
    tm = min(512, N)
    scale = 1.0 / float(np.sqrt(HEAD_DIM))

    gq = (jnp.tile(q_norm_w, N_HEADS) * scale)[None, :].astype(F32)
    gk = jnp.tile(k_norm_w, N_HEADS)[None, :].astype(F32)
    gc = gn_conv_w.reshape(1, -1).astype(F32)
    ga = gn_attn_w.reshape(1, -1).astype(F32)
    yc, q, kt, v = _proj_call(x, norm_mix_w[None, :], w_in.astype(BF16), conv_w, gq, gk, gc, ts=ts)
    ya = _attn_call(q, kt, v, _suffix_matrix(), ga, tq=tq)

    pad_lanes = lambda a: jnp.pad(a.astype(F32), ((0, 0), (0, LANES - a.shape[1])))
    wr = jnp.concatenate([pad_lanes(w_expert), pad_lanes(w_group)], axis=1)
    br = jnp.concatenate([pad_lanes(b_expert[None, :]), pad_lanes(b_group[None, :])], axis=1)
    wr_hi = wr.astype(BF16)
    wr_lo = (wr - wr_hi.astype(F32)).astype(BF16)
    rc = min(ROUTE_CHUNK, tm)
    tri = jnp.asarray(np.tril(np.ones((rc, rc), np.float32), -1), BF16)
    x1, xn2, meta, counts = _mix_call(
        yc.reshape(N, -1), ya.reshape(N, -1), x.reshape(N, D), w_out.astype(BF16),
        norm_ffn_w[None, :], wr_hi, wr_lo, br, tri, tm=tm)

    eid = meta[:, 0:TOP_K].astype(jnp.int32).reshape(-1)
    rank = meta[:, TOP_K:2 * TOP_K].astype(jnp.int32).reshape(-1)
    cnt = counts[0, :N_EXPERTS].astype(jnp.int32)
    pcnt = (cnt + DISPATCH_BLOCK - 1) // DISPATCH_BLOCK * DISPATCH_BLOCK
    pend = jnp.cumsum(pcnt)
    pstart = pend - pcnt
    n_blocks = -(-N * TOP_K // DISPATCH_BLOCK) + N_EXPERTS
    blk_start = jnp.arange(n_blocks, dtype=jnp.int32) * DISPATCH_BLOCK
    blk_e = jnp.sum((blk_start[:, None] >= pend[None, :]).astype(jnp.int32), axis=1)
    blk_e = jnp.minimum(blk_e, N_EXPERTS - 1)
    n_used = pend[-1:] // DISPATCH_BLOCK
    onehot = eid[:, None] == jnp.arange(N_EXPERTS, dtype=jnp.int32)[None, :]
    dest = rank + jnp.sum(jnp.where(onehot, pstart[None, :], 0), axis=1)

    xin = _dispatch_call(pstart, cnt, pend, dest, xn2, n_blocks * DISPATCH_BLOCK, td=min(512, N))
    y = _ffn_call(blk_e, n_used, xin, w1, w3, w2)
    out = _combine_call(dest, y, x1, meta, tc=min(256, N))
    return out.reshape(B, S, D)


def kernel(x, norm_mix_w, w_in, conv_w, q_norm_w, k_norm_w, group_norm_conv_w, group_norm_attn_w, w_out,
           norm_ffn_w, w_group, b_group, w_expert, b_expert, w1, w3, w2):
    depth = norm_mix_w.shape[0]
    for l in range(depth):
        x = _layer(x, norm_mix_w[l], w_in[l], conv_w[l], q_norm_w[l], k_norm_w[l], group_norm_conv_w[l],
                   group_norm_attn_w[l], w_out[l], norm_ffn_w[l], w_group[l], b_group[l], w_expert[l],
                   b_expert[l], w1[l], w3[l], w2[l])
    return x
```

```python
import functools

import numpy as np
import jax
import jax.numpy as jnp
from jax import lax
from jax.experimental import pallas as pl
from jax.experimental.pallas import tpu as pltpu

F32 = jnp.float32
BF16 = jnp.bfloat16

EPS = 1e-6
MASKED = -1e30
STICK_GONE = 110.0
LANES = 128
HEAD_DIM = 64
N_HEADS = 8
N_CONV_GROUPS = 8
CONV_K = 3
N_GROUPS = 4
EXPERTS_PER_GROUP = 8
N_EXPERTS = N_GROUPS * EXPERTS_PER_GROUP
TOP_K = 2
DISPATCH_BLOCK = 512
HEAD_PAIRS = N_HEADS * HEAD_DIM // LANES

ROW_TILE = 8
KEY_BLOCK = 256
DMA_UNROLL = 4
N_DMA_QUEUES = 2
VMEM_LIMIT = 56 * 1024 * 1024


def _pair_group_rms(x, lane_lo):
    x2 = x * x
    s_lo = jnp.sum(jnp.where(lane_lo, x2, 0.0), axis=-1, keepdims=True)
    s_hi = jnp.sum(jnp.where(lane_lo, 0.0, x2), axis=-1, keepdims=True)
    ms = jnp.where(lane_lo, s_lo, s_hi) * (1.0 / HEAD_DIM)
    return x * lax.rsqrt(ms + EPS)


def _proj_kernel(x_ref, g1_ref, win_ref, cw_ref, gq_ref, gk_ref, gc_ref,
                 yc_ref, q_ref, kt_ref, v_ref, ubuf, *, ts, width):
    i = pl.program_id(1)
    x = x_ref[0]
    ms = jnp.mean(x * x, axis=-1, keepdims=True)
    xn = (x * lax.rsqrt(ms + EPS) * g1_ref[...]).astype(BF16)

    def proj(c):
        return jnp.dot(xn, win_ref[:, c * width:(c + 1) * width], preferred_element_type=F32)

    lane_lo = lax.broadcasted_iota(jnp.int32, (ts, LANES), 1) < HEAD_DIM

    u = proj(1) * proj(2)

    @pl.when(i == 0)
    def _():
        ubuf[0:8, :] = jnp.zeros((8, width), F32)

    ubuf[8:8 + ts, :] = u
    u1 = ubuf[7:7 + ts, :]
    u2 = ubuf[6:6 + ts, :]
    ubuf[0:8, :] = u[ts - 8:ts, :]
    y = cw_ref[0:1, :] * u2 + cw_ref[1:2, :] * u1 + cw_ref[2:3, :] * u
    yc = proj(0) * y
    for c in range(width // LANES):
        sl = slice(c * LANES, (c + 1) * LANES)
        yc_ref[0, :, sl] = (_pair_group_rms(yc[:, sl], lane_lo) * gc_ref[:, sl]).astype(BF16)

    q = proj(3)
    for c in range(width // LANES):
        sl = slice(c * LANES, (c + 1) * LANES)
        q_ref[0, :, sl] = (_pair_group_rms(q[:, sl], lane_lo) * gq_ref[:, sl]).astype(BF16)

    k = proj(4)
    for c in range(width // LANES):
        sl = slice(c * LANES, (c + 1) * LANES)
        kn = _pair_group_rms(k[:, sl], lane_lo) * gk_ref[:, sl]
        knt = kn.T.astype(BF16)
        for j in range(ts // KEY_BLOCK):
            kt_ref[0, c, j] = knt[:, j * KEY_BLOCK:(j + 1) * KEY_BLOCK]

    v_ref[0] = proj(5).astype(BF16)


def _proj_call(x, g1, w_in, conv_w, gq, gk, gc, *, ts):
    B, S, D = x.shape
    width = w_in.shape[1] // 6
    nkb = S // KEY_BLOCK
    kern = functools.partial(_proj_kernel, ts=ts, width=width)
    const = lambda b, i: (0, 0)
    return pl.pallas_call(
        kern,
        grid=(B, S // ts),
        in_specs=[
            pl.BlockSpec((1, ts, D), lambda b, i: (b, i, 0)),
            pl.BlockSpec((1, D), const),
            pl.BlockSpec(w_in.shape, const),
            pl.BlockSpec(conv_w.shape, const),
            pl.BlockSpec((1, width), const),
            pl.BlockSpec((1, width), const),
            pl.BlockSpec((1, width), const),
        ],
        out_specs=[
            pl.BlockSpec((1, ts, width), lambda b, i: (b, i, 0)),
            pl.BlockSpec((1, ts, width), lambda b, i: (b, i, 0)),
            pl.BlockSpec((1, HEAD_PAIRS, ts // KEY_BLOCK, LANES, KEY_BLOCK), lambda b, i: (b, 0, i, 0, 0)),
            pl.BlockSpec((1, ts, width), lambda b, i: (b, i, 0)),
        ],
        out_shape=[
            jax.ShapeDtypeStruct((B, S, width), BF16),
            jax.ShapeDtypeStruct((B, S, width), BF16),
            jax.ShapeDtypeStruct((B, HEAD_PAIRS, nkb, LANES, KEY_BLOCK), BF16),
            jax.ShapeDtypeStruct((B, S, width), BF16),
        ],
        scratch_shapes=[pltpu.VMEM((ts + 8, width), F32)],
        compiler_params=pltpu.CompilerParams(
            dimension_semantics=("arbitrary", "arbitrary"), vmem_limit_bytes=VMEM_LIMIT),
        name="proj_conv_qknorm",
    )(x, g1, w_in, conv_w, gq, gk, gc)


def _attn_kernel(q_ref, kt_ref, v_ref, tm_ref, ga_ref, o_ref, acc_ref, carry_ref, left_ref, *, tq):
    qi = pl.program_id(2)
    lane_lo = lax.broadcasted_iota(jnp.int32, (tq, LANES), 1) < HEAD_DIM
    q = q_ref[0]
    zero = jnp.zeros_like(q)
    qh = (jnp.where(lane_lo, q, zero), jnp.where(lane_lo, zero, q))
    n_sub = tq // KEY_BLOCK

    def key_block(j, s, diagonal):
        rows = slice(s * KEY_BLOCK, (s + 1) * KEY_BLOCK)
        kt = kt_ref[0, 0, j]
        start = pl.multiple_of(j * KEY_BLOCK, KEY_BLOCK)
        vb = v_ref[0, pl.ds(start, KEY_BLOCK), :]
        for h in range(2):
            z = jnp.dot(qh[h][rows, :], kt, preferred_element_type=F32)
            sp = jnp.maximum(jnp.log(1.0 + jnp.exp(jnp.minimum(z, 80.0))), z)
            if diagonal:
                row = lax.broadcasted_iota(jnp.int32, (KEY_BLOCK, KEY_BLOCK), 0)
                col = lax.broadcasted_iota(jnp.int32, (KEY_BLOCK, KEY_BLOCK), 1)
                sp = jnp.where(col < row, sp, 0.0)
                z = jnp.where(col < row, z, MASKED)
            r = jnp.dot(sp.astype(BF16), tm_ref[...], preferred_element_type=F32)
            total = jnp.broadcast_to(r[:, 0:1], (KEY_BLOCK, LANES))
            if diagonal:
                a = jnp.exp(z - r)
                carry = total
            else:
                carry = carry_ref[h, rows, :]
                a = jnp.exp(z - (r + jnp.concatenate([carry] * (KEY_BLOCK // LANES), axis=1)))
                carry = carry + total
            carry_ref[h, rows, :] = carry
            left_ref[h * n_sub + s:h * n_sub + s + 1, :] = jnp.min(carry, axis=0, keepdims=True)
            av = jnp.dot(a.astype(BF16), vb, preferred_element_type=F32)
            if diagonal:
                acc_ref[h, rows, :] = av
            else:
                acc_ref[h, rows, :] += av

    first = qi * n_sub

    @pl.when(qi == 0)
    def _():
        key_block(0, 0, True)
        for s in range(1, n_sub):
            key_block(s, s, True)
            key_block(s - 1, s, False)

    @pl.when(qi > 0)
    def _():
        for s in range(n_sub):
            key_block(first + s, s, True)
            key_block(first + s - 1, s, False)

    def stick_left(t):
        return jnp.logical_and(first + n_sub - 3 - t >= 0, jnp.min(left_ref[...]) < STICK_GONE)

    def step(t):
        for s in range(n_sub):
            j = first + s - 2 - t

            @pl.when(j >= 0)
            def _():
                key_block(j, s, False)
        return t + 1

    lax.while_loop(stick_left, step, 0)

    o = jnp.where(lane_lo, acc_ref[0], acc_ref[1])
    o_ref[0] = (_pair_group_rms(o, lane_lo) * ga_ref[...]).astype(BF16)


def _attn_call(q, kt, v, tmat, ga, *, tq):
    B, S, width = q.shape
    assert tq % KEY_BLOCK == 0 and S % tq == 0
    kern = functools.partial(_attn_kernel, tq=tq)
    return pl.pallas_call(
        kern,
        grid=(B, HEAD_PAIRS, S // tq),
        in_specs=[
            pl.BlockSpec((1, tq, LANES), lambda b, p, i: (b, i, p)),
            pl.BlockSpec((1, 1) + kt.shape[2:], lambda b, p, i: (b, p, 0, 0, 0)),
            pl.BlockSpec((1, S, LANES), lambda b, p, i: (b, 0, p)),
            pl.BlockSpec(tmat.shape, lambda b, p, i: (0, 0)),
            pl.BlockSpec((1, LANES), lambda b, p, i: (0, p)),
        ],
        out_specs=pl.BlockSpec((1, tq, LANES), lambda b, p, i: (b, i, p)),
        out_shape=jax.ShapeDtypeStruct((B, S, width), BF16),
        scratch_shapes=[pltpu.VMEM((2, tq, LANES), F32), pltpu.VMEM((2, tq, LANES), F32),
                        pltpu.VMEM((2 * (tq // KEY_BLOCK), LANES), F32)],
        compiler_params=pltpu.CompilerParams(
            dimension_semantics=("arbitrary", "arbitrary", "arbitrary"), vmem_limit_bytes=VMEM_LIMIT),
        name="stickbreak_attn",
    )(q, kt, v, tmat, ga)


def _mix_kernel(yc_ref, ya_ref, x_ref, wo_ref, g2_ref, wrh_ref, wrl_ref, br_ref, tri_ref,
                x1_ref, xn_ref, meta_ref, cnt_ref, xn_prev, *, tm, width):
    i = pl.program_id(0)
    lane = lax.broadcasted_iota(jnp.int32, (tm, LANES), 1)
    neg = jnp.float32(-jnp.inf)

    def first_argmax(vals):
        m = jnp.max(vals, axis=-1, keepdims=True)
        idx = jnp.min(jnp.where(vals == m, lane, LANES), axis=-1, keepdims=True)
        return m, idx

    @pl.when(i == 0)
    def _():
        cnt_ref[...] = jnp.zeros_like(cnt_ref)
        xn_prev[...] = jnp.zeros_like(xn_prev)

    xp = xn_prev[...]
    xh = xp.astype(BF16)
    xl = (xp - xh.astype(F32)).astype(BF16)
    logits = (jnp.dot(xh, wrh_ref[...], preferred_element_type=F32)
              + jnp.dot(xl, wrh_ref[...], preferred_element_type=F32)
              + jnp.dot(xh, wrl_ref[...], preferred_element_type=F32)) + br_ref[...]
    el = logits[:, :LANES]
    gl = logits[:, LANES:]
    glm = jnp.where(lane < N_GROUPS, gl, neg)
    gmax, gidx = first_argmax(glm)
    p_g = 1.0 / jnp.sum(jnp.exp(glm - gmax), axis=-1, keepdims=True)
    elm = jnp.where((lane // EXPERTS_PER_GROUP) == gidx, el, neg)
    v1, i1 = first_argmax(elm)
    v2, i2 = first_argmax(jnp.where(lane == i1, neg, elm))
    w = jnp.exp(v2 - v1)
    gate1 = p_g / (1.0 + w)
    gate2 = p_g * w / (1.0 + w)

    o1 = lane == i1
    o2 = lane == i2
    osum = jnp.where(jnp.logical_or(o1, o2), 1.0, 0.0)
    seen = cnt_ref[...]
    base = jnp.dot(tri_ref[...], osum.astype(BF16), preferred_element_type=F32) + seen
    rank1 = jnp.sum(jnp.where(o1, base, 0.0), axis=-1, keepdims=True)
    rank2 = jnp.sum(jnp.where(o2, base, 0.0), axis=-1, keepdims=True)
    counted = jnp.where(i > 0, 1.0, 0.0)
    cnt_ref[...] = seen + counted * jnp.sum(osum, axis=0, keepdims=True)

    cols = (i1.astype(F32), i2.astype(F32), rank1, rank2, gate1, gate2)
    meta = jnp.zeros((tm, LANES), F32)
    for n, cval in enumerate(cols):
        meta = jnp.where(lane == n, cval, meta)
    meta_ref[...] = meta

    mix = jnp.dot(yc_ref[...], wo_ref[0:width, :], preferred_element_type=F32)
    mix = mix + jnp.dot(ya_ref[...], wo_ref[width:2 * width, :], preferred_element_type=F32)
    x1 = x_ref[...] + mix
    x1_ref[...] = x1
    ms = jnp.mean(x1 * x1, axis=-1, keepdims=True)
    xn = x1 * lax.rsqrt(ms + EPS) * g2_ref[...]
    xn_ref[...] = xn.reshape(tm, ROW_TILE, LANES)
    xn_prev[...] = xn


def _mix_call(yc, ya, x2d, w_out, g2, wr_hi, wr_lo, br, tri, *, tm):
    N, D = x2d.shape
    width = yc.shape[1]
    n_tiles = N // tm
    kern = functools.partial(_mix_kernel, tm=tm, width=width)
    cur = lambda i: (jnp.minimum(i, n_tiles - 1), 0)
    prev = lambda i: (jnp.maximum(i - 1, 0), 0)
    const = lambda i: (0, 0)
    return pl.pallas_call(
        kern,
        grid=(n_tiles + 1,),
        in_specs=[
            pl.BlockSpec((tm, width), cur),
            pl.BlockSpec((tm, width), cur),
            pl.BlockSpec((tm, D), cur),
            pl.BlockSpec(w_out.shape, const),
            pl.BlockSpec((1, D), const),
            pl.BlockSpec(wr_hi.shape, const),
            pl.BlockSpec(wr_lo.shape, const),
            pl.BlockSpec(br.shape, const),
            pl.BlockSpec(tri.shape, const),
        ],
        out_specs=[
            pl.BlockSpec((tm, D), cur),
            pl.BlockSpec((tm, ROW_TILE, LANES), lambda i: (jnp.minimum(i, n_tiles - 1), 0, 0)),
            pl.BlockSpec((tm, LANES), prev),
            pl.BlockSpec((1, LANES), const),
        ],
        out_shape=[
            jax.ShapeDtypeStruct((N, D), F32),
            jax.ShapeDtypeStruct((N, ROW_TILE, LANES), F32),
            jax.ShapeDtypeStruct((N, LANES), F32),
            jax.ShapeDtypeStruct((1, LANES), F32),
        ],
        scratch_shapes=[pltpu.VMEM((tm, D), F32)],
        compiler_params=pltpu.CompilerParams(
            dimension_semantics=("arbitrary",), vmem_limit_bytes=VMEM_LIMIT),
        name="outproj_router",
    )(yc, ya, x2d, w_out, g2, wr_hi, wr_lo, br, tri)


def _dispatch_kernel(pstart_ref, cnt_ref, pend_ref, dest_ref, xn_ref, xin_ref, zblk, sem, zsem, *, td):
    del pstart_ref
    n_blocks = xin_ref.shape[0] // DISPATCH_BLOCK

    @pl.when(pl.program_id(0) == 0)
    def _():
        zblk[...] = jnp.zeros_like(zblk)

        def zero_block(b):
            start = pl.multiple_of(b * DISPATCH_BLOCK, DISPATCH_BLOCK)
            return pltpu.make_async_copy(zblk, xin_ref.at[pl.ds(start, DISPATCH_BLOCK)], zsem)

        def padded_blocks(fn):
            def per_expert(e, _):
                @pl.when(cnt_ref[e] % DISPATCH_BLOCK != 0)
                def _():
                    fn(zero_block(pend_ref[e] // DISPATCH_BLOCK - 1))
                return 0

            def unused(b, _):
                fn(zero_block(b))
                return 0

            lax.fori_loop(0, N_EXPERTS, per_expert, 0)
            lax.fori_loop(pend_ref[N_EXPERTS - 1] // DISPATCH_BLOCK, n_blocks, unused, 0)

        padded_blocks(lambda copy: copy.start())
        padded_blocks(lambda copy: copy.wait())

    def row_copy(r, k):
        return pltpu.make_async_copy(xn_ref.at[r], xin_ref.at[dest_ref[TOP_K * r + k]], sem)

    def start_rows(r, _):
        for k in range(TOP_K):
            row_copy(r, k).start(priority=k % N_DMA_QUEUES)
        return 0

    def wait_rows(r, _):
        for k in range(TOP_K):
            row_copy(r, k).wait()
        return 0

    lax.fori_loop(0, td, start_rows, 0, unroll=DMA_UNROLL)
    lax.fori_loop(0, td, wait_rows, 0, unroll=DMA_UNROLL)


def _dispatch_call(pstart, cnt, pend, dest_flat, xn, n_rows, *, td):
    N = xn.shape[0]
    kern = functools.partial(_dispatch_kernel, td=td)
    smem_block = pl.BlockSpec((TOP_K * td,), lambda i, *_: (i,), memory_space=pltpu.SMEM)
    grid_spec = pltpu.PrefetchScalarGridSpec(
        num_scalar_prefetch=3,
        grid=(N // td,),
        in_specs=[smem_block, pl.BlockSpec((td, ROW_TILE, LANES), lambda i, *_: (i, 0, 0))],
        out_specs=pl.BlockSpec(memory_space=pl.ANY),
        scratch_shapes=[pltpu.VMEM((DISPATCH_BLOCK, ROW_TILE, LANES), xn.dtype), pltpu.SemaphoreType.DMA(()),
                        pltpu.SemaphoreType.DMA(())],
    )
    return pl.pallas_call(
        kern,
        grid_spec=grid_spec,
        out_shape=jax.ShapeDtypeStruct((n_rows,) + xn.shape[1:], xn.dtype),
        compiler_params=pltpu.CompilerParams(dimension_semantics=("arbitrary",)),
        name="moe_dispatch",
    )(pstart, cnt, pend, dest_flat, xn)


def _ffn_kernel(blk_e_ref, n_used_ref, x_ref, w1_ref, w3_ref, w2_ref, y_ref, w1b, w3b, w2b):
    i = pl.program_id(0)
    prev = blk_e_ref[jnp.maximum(i - 1, 0)]
    changed = jnp.logical_or(i == 0, blk_e_ref[i] != prev)

    @pl.when(changed)
    def _():
        w1b[...] = w1_ref[...].astype(BF16)
        w3b[...] = w3_ref[...].astype(BF16)
        w2b[...] = w2_ref[...].astype(BF16)

    @pl.when(i < n_used_ref[0])
    def _():
        x = x_ref[...].reshape(DISPATCH_BLOCK, ROW_TILE * LANES).astype(BF16)
        a = jnp.dot(x, w1b[...], preferred_element_type=F32)
        b = jnp.dot(x, w3b[...], preferred_element_type=F32)
        h = (a * jax.nn.sigmoid(a) * b).astype(BF16)
        y = jnp.dot(h, w2b[...], preferred_element_type=F32)
        y_ref[...] = y.reshape(DISPATCH_BLOCK, ROW_TILE, LANES)

    @pl.when(i >= n_used_ref[0])
    def _():
        y_ref[...] = jnp.zeros_like(y_ref)


def _ffn_call(blk_e, n_used, xin, w1, w3, w2):
    P = xin.shape[0]
    E, D, DE = w1.shape
    n_blocks = P // DISPATCH_BLOCK
    tok_block = (DISPATCH_BLOCK, ROW_TILE, LANES)
    grid_spec = pltpu.PrefetchScalarGridSpec(
        num_scalar_prefetch=2,
        grid=(n_blocks,),
        in_specs=[
            pl.BlockSpec(tok_block, lambda i, be, nu: (jnp.minimum(i, nu[0] - 1), 0, 0)),
            pl.BlockSpec((None, D, DE), lambda i, be, nu: (be[i], 0, 0)),
            pl.BlockSpec((None, D, DE), lambda i, be, nu: (be[i], 0, 0)),
            pl.BlockSpec((None, DE, D), lambda i, be, nu: (be[i], 0, 0)),
        ],
        out_specs=pl.BlockSpec(tok_block, lambda i, be, nu: (i, 0, 0)),
        scratch_shapes=[pltpu.VMEM((D, DE), BF16), pltpu.VMEM((D, DE), BF16), pltpu.VMEM((DE, D), BF16)],
    )
    return pl.pallas_call(
        _ffn_kernel,
        grid_spec=grid_spec,
        out_shape=jax.ShapeDtypeStruct(xin.shape, F32),
        compiler_params=pltpu.CompilerParams(
            dimension_semantics=("arbitrary",), vmem_limit_bytes=VMEM_LIMIT),
        name="moe_expert_ffn",
    )(blk_e, n_used, xin, w1, w3, w2)


def _combine_kernel(dest_ref, dest_next_ref, y_ref, x1_ref, meta_ref, o_ref, ybuf, sems, *, tc):
    i = pl.program_id(0)
    slot = i % 2

    def gather(d_ref, s):
        def start(r, _):
            for k in range(TOP_K):
                pltpu.make_async_copy(y_ref.at[d_ref[TOP_K * r + k]], ybuf.at[s, k, r], sems.at[s]).start(
                    priority=k % N_DMA_QUEUES)
            return 0
        lax.fori_loop(0, tc, start, 0, unroll=DMA_UNROLL)

    @pl.when(i == 0)
    def _():
        gather(dest_ref, 0)

    @pl.when(i + 1 < pl.num_programs(0))
    def _():
        gather(dest_next_ref, 1 - slot)

    def wait(r, _):
        for k in range(TOP_K):
            pltpu.make_async_copy(y_ref.at[0], ybuf.at[slot, k, r], sems.at[slot]).wait()
        return 0

    lax.fori_loop(0, tc, wait, 0, unroll=DMA_UNROLL)
    meta = meta_ref[...]
    y1 = ybuf[slot, 0].reshape(tc, ROW_TILE * LANES)
    y2 = ybuf[slot, 1].reshape(tc, ROW_TILE * LANES)
    o_ref[...] = x1_ref[...] + (y1 * meta[:, 4:5] + y2 * meta[:, 5:6])


def _combine_call(dest_flat, y, x1, meta, *, tc):
    N, D = x1.shape
    n_tiles = N // tc
    kern = functools.partial(_combine_kernel, tc=tc)
    return pl.pallas_call(
        kern,
        grid=(n_tiles,),
        in_specs=[
            pl.BlockSpec((TOP_K * tc,), lambda i: (i,), memory_space=pltpu.SMEM),
            pl.BlockSpec((TOP_K * tc,), lambda i: (jnp.minimum(i + 1, n_tiles - 1),), memory_space=pltpu.SMEM),
            pl.BlockSpec(memory_space=pl.ANY),
            pl.BlockSpec((tc, D), lambda i: (i, 0)),
            pl.BlockSpec((tc, LANES), lambda i: (i, 0)),
        ],
        out_specs=pl.BlockSpec((tc, D), lambda i: (i, 0)),
        out_shape=jax.ShapeDtypeStruct((N, D), F32),
        scratch_shapes=[pltpu.VMEM((2, TOP_K, tc, ROW_TILE, LANES), F32), pltpu.SemaphoreType.DMA((2,))],
        compiler_params=pltpu.CompilerParams(
            dimension_semantics=("arbitrary",), vmem_limit_bytes=VMEM_LIMIT),
        name="moe_combine",
    )(dest_flat, dest_flat, y, x1, meta)


def _suffix_matrix():
    r = np.arange(KEY_BLOCK)[:, None]
    c = np.arange(KEY_BLOCK)[None, :]
    return jnp.asarray(r >= c, BF16)


def _layer(x, norm_mix_w, w_in, conv_w, q_norm_w, k_norm_w, gn_conv_w, gn_attn_w, w_out,
           norm_ffn_w, w_group, b_group, w_expert, b_expert, w1, w3, w2):
    B, S, D = x.shape
    N = B * S
    ts = min(512, S)
    tq = min(4 * KEY_BLOCK, S)
    tm = min(512, N)
    scale = 1.0 / float(np.sqrt(HEAD_DIM))

    gq = (jnp.tile(q_norm_w, N_HEADS) * scale)[None, :].astype(F32)
    gk = jnp.tile(k_norm_w, N_HEADS)[None, :].astype(F32)
    gc = gn_conv_w.reshape(1, -1).astype(F32)
    ga = gn_attn_w.reshape(1, -1).astype(F32)
    yc, q, kt, v = _proj_call(x, norm_mix_w[None, :], w_in.astype(BF16), conv_w, gq, gk, gc, ts=ts)
    ya = _attn_call(q, kt, v, _suffix_matrix(), ga, tq=tq)

    pad_lanes = lambda a: jnp.pad(a.astype(F32), ((0, 0), (0, LANES - a.shape[1])))
    wr = jnp.concatenate([pad_lanes(w_expert), pad_lanes(w_group)], axis=1)
    br = jnp.concatenate([pad_lanes(b_expert[None, :]), pad_lanes(b_group[None, :])], axis=1)
    wr_hi = wr.astype(BF16)
    wr_lo = (wr - wr_hi.astype(F32)).astype(BF16)
    tri = jnp.asarray(np.tril(np.ones((tm, tm), np.float32), -1), BF16)
    x1, xn2, meta, counts = _mix_call(
        yc.reshape(N, -1), ya.reshape(N, -1), x.reshape(N, D), w_out.astype(BF16),
        norm_ffn_w[None, :], wr_hi, wr_lo, br, tri, tm=tm)

    eid = meta[:, 0:TOP_K].astype(jnp.int32).reshape(-1)
    rank = meta[:, TOP_K:2 * TOP_K].astype(jnp.int32).reshape(-1)
    cnt = counts[0, :N_EXPERTS].astype(jnp.int32)
    pcnt = (cnt + DISPATCH_BLOCK - 1) // DISPATCH_BLOCK * DISPATCH_BLOCK
    pend = jnp.cumsum(pcnt)
    pstart = pend - pcnt
    n_blocks = -(-N * TOP_K // DISPATCH_BLOCK) + N_EXPERTS
    blk_start = jnp.arange(n_blocks, dtype=jnp.int32) * DISPATCH_BLOCK
    blk_e = jnp.sum((blk_start[:, None] >= pend[None, :]).astype(jnp.int32), axis=1)
    blk_e = jnp.minimum(blk_e, N_EXPERTS - 1)
    n_used = pend[-1:] // DISPATCH_BLOCK
    onehot = eid[:, None] == jnp.arange(N_EXPERTS, dtype=jnp.int32)[None, :]
    dest = rank + jnp.sum(jnp.where(onehot, pstart[None, :], 0), axis=1)

    xin = _dispatch_call(pstart, cnt, pend, dest, xn2, n_blocks * DISPATCH_BLOCK, td=min(512, N))
    y = _ffn_call(blk_e, n_used, xin, w1, w3, w2)
    out = _combine_call(dest, y, x1, meta, tc=min(256, N))
    return out.reshape(B, S, D)


def kernel(x, norm_mix_w, w_in, conv_w, q_norm_w, k_norm_w, group_norm_conv_w, group_norm_attn_w, w_out,
           norm_ffn_w, w_group, b_group, w_expert, b_expert, w1, w3, w2):
    depth = norm_mix_w.shape[0]
    for l in range(depth):
        x = _layer(x, norm_mix_w[l], w_in[l], conv_w[l], q_norm_w[l], k_norm_w[l], group_norm_conv_w[l],
                   group_norm_attn_w[l], w_out[l], norm_ffn_w[l], w_group[l], b_group[l], w_expert[l],
                   b_expert[l], w1[l], w3[l], w2[l])
    return x
```
